```python
import math
import jax, jax.numpy as jnp
from jax import lax
import numpy as np

D_MODEL = 2048
BATCH = 4
SEQ = 4096
DEPTH = 4

N_MIXERS = 2
N_META = 16
FNET_GROUPS = 8
SSD_EXPAND = 2
D_INNER = SSD_EXPAND * D_MODEL
SSD_HEAD_DIM = 64
SSD_HEADS = D_INNER // SSD_HEAD_DIM
SSD_GROUPS = 8
SSD_STATE = 128
CONV_WIDTH = 5
CHUNK = 256
GN = SSD_GROUPS * SSD_STATE
CONV_DIM = D_INNER + 2 * GN
D_IN_PROJ = D_INNER + CONV_DIM + 2 * SSD_HEADS
D_FF = (((8 * D_MODEL + 2) // 3 + 255) // 256) * 256
N_FNET = (DEPTH + 1) // 2
N_SSD = DEPTH // 2
EPS = 1e-6

kernel_name = "hybrid_fnet_ssd_meta_encoder"


def rms_norm(x, w):
    xf = x.astype(jnp.float32)
    y = xf * lax.rsqrt(jnp.mean(xf * xf, axis=-1, keepdims=True) + EPS)
    return (y * w.astype(jnp.float32)).astype(x.dtype)


def gated_group_rms_norm(y, z, w):
    b, L, dn = y.shape
    g = (y.astype(jnp.float32) * jax.nn.silu(z.astype(jnp.float32))).reshape(b, L, SSD_GROUPS, dn // SSD_GROUPS)
    g = g * lax.rsqrt(jnp.mean(g * g, axis=-1, keepdims=True) + EPS)
    return (g.reshape(b, L, dn) * w.astype(jnp.float32)).astype(z.dtype)


def fourier_mixer(u, w_out):
    b, L, d = u.shape
    ug = u.astype(jnp.float32).reshape(b, L, FNET_GROUPS, d // FNET_GROUPS)
    f = jnp.fft.fftn(ug, axes=(1, 3), norm="ortho").real.reshape(b, L, d)
    return f.astype(u.dtype) @ w_out


def centred_depthwise_conv(u, w, bias):
    k = w.shape[0]
    out = lax.conv_general_dilated(
        u, w.astype(u.dtype)[:, None, :], window_strides=(1,),
        padding=[(k // 2, k // 2)], dimension_numbers=("NWC", "WIO", "NWC"),
        feature_group_count=u.shape[-1])
    return out + bias.astype(u.dtype)


def segsum(a):
    t = a.shape[-1]
    cs = jnp.cumsum(a, axis=-1)
    diff = cs[..., :, None] - cs[..., None, :]
    mask = jnp.tril(jnp.ones((t, t), dtype=bool))
    return jnp.where(mask, diff, -jnp.inf)


def ssd_chunked(xdt, dA, Bm, Cm):
    b, lp, h, p = xdt.shape
    g, n = Bm.shape[2], Bm.shape[3]
    e = h // g
    c = lp // CHUNK
    X = xdt.reshape(b, c, CHUNK, g, e, p)
    Bc = Bm.reshape(b, c, CHUNK, g, n)
    Cc = Cm.reshape(b, c, CHUNK, g, n)
    A = dA.reshape(b, c, CHUNK, g, e).transpose(0, 3, 4, 1, 2)
    A_cs = jnp.cumsum(A, axis=-1)
    CB = jnp.einsum("bclgn,bcsgn->bgcls", Cc, Bc)
    M = jnp.exp(segsum(A)) * CB[:, :, None]
    y_diag = jnp.einsum("bgecls,bcsgep->bclgep", M, X)
    decay_states = jnp.exp(A_cs[..., -1:] - A_cs)
    states = jnp.einsum("bclgn,bgecl,bclgep->bcgepn", Bc, decay_states, X)
    states = jnp.concatenate([jnp.zeros_like(states[:, :1]), states], axis=1)
    a_last = jnp.pad(A_cs[..., -1], ((0, 0), (0, 0), (0, 0), (1, 0)))
    decay_chunk = jnp.exp(segsum(a_last))
    new_states = jnp.einsum("bgezc,bcgepn->bzgepn", decay_chunk, states)
    prev_states = new_states[:, :-1]
    y_off = jnp.einsum("bclgn,bcgepn,bgecl->bclgep", Cc, prev_states, jnp.exp(A_cs))
    return (y_diag + y_off).reshape(b, lp, h, p)


def bidir_ssd_mixer(u, w_in, conv_w, conv_b, dt_bias, a_log, d_skip, norm_w, w_out):
    b, L, _ = u.shape
    proj = u @ w_in
    z = proj[..., :D_INNER]
    xbc = proj[..., D_INNER:D_INNER + CONV_DIM]
    dt_raw = proj[..., D_INNER + CONV_DIM:].reshape(b, L, 2, SSD_HEADS)
    xbc = jax.nn.silu(centred_depthwise_conv(xbc, conv_w, conv_b)).astype(jnp.float32)
    xs = xbc[..., :D_INNER].reshape(b, L, SSD_HEADS, SSD_HEAD_DIM)
    Bm = xbc[..., D_INNER:D_INNER + GN].reshape(b, L, SSD_GROUPS, SSD_STATE)
    Cm = xbc[..., D_INNER + GN:].reshape(b, L, SSD_GROUPS, SSD_STATE)
    dt = jax.nn.softplus(dt_raw.astype(jnp.float32) + dt_bias.astype(jnp.float32))
    A = -jnp.exp(a_log.astype(jnp.float32))
    front = CHUNK - N_META
    tail = (-(L - N_META)) % CHUNK
    pad = lambda t: jnp.pad(t, [(0, 0), (front, tail)] + [(0, 0)] * (t.ndim - 2))
    rev = lambda t: jnp.flip(t, axis=1)
    xs_p, B_p, C_p, dt_p = pad(xs), pad(Bm), pad(Cm), pad(dt)
    dt_f, dt_b = dt_p[:, :, 0], dt_p[:, :, 1]
    y_f = ssd_chunked(xs_p * dt_f[..., None], dt_f * A[0], B_p, C_p)
    y_b = rev(ssd_chunked(rev(xs_p * dt_b[..., None]), rev(dt_b * A[1]), rev(B_p), rev(C_p)))
    y = (y_f + y_b)[:, front:front + L] + xs * d_skip.astype(jnp.float32)[:, None]
    y = y.reshape(b, L, D_INNER)
    return gated_group_rms_norm(y, z, norm_w) @ w_out


def swiglu_ffn(u, w_gate, w_up, w_down):
    return (jax.nn.silu(u @ w_gate) * (u @ w_up)) @ w_down


def setup_inputs(seed: int = 0) -> dict:
    key = jax.random.key(seed)
    ks = jax.random.split(key, 18)
    f32 = jnp.float32
    nrm = lambda k, shape, s: jax.random.normal(k, shape, f32) * s
    dt0 = jnp.exp(jax.random.uniform(ks[9], (N_SSD, 2, SSD_HEADS), f32, math.log(1e-3), math.log(1e-1)))
    return {
        "x": nrm(ks[0], (BATCH, SEQ, D_MODEL), 1.0),
        "meta_tokens": nrm(ks[1], (N_META, D_MODEL), 1.0),
        "norm_mix_w": 1.0 + nrm(ks[2], (DEPTH, D_MODEL), 0.02),
        "norm_ffn_w": 1.0 + nrm(ks[3], (DEPTH, D_MODEL), 0.02),
        "norm_final_w": 1.0 + nrm(ks[4], (D_MODEL,), 0.02),
        "fnet_w_out": nrm(ks[5], (N_FNET, D_MODEL, D_MODEL), D_MODEL ** -0.5),
        "ssd_w_in": nrm(ks[6], (N_SSD, D_MODEL, D_IN_PROJ), D_MODEL ** -0.5),
        "ssd_conv_w": nrm(ks[7], (N_SSD, CONV_WIDTH, CONV_DIM), CONV_WIDTH ** -0.5),
        "ssd_conv_b": nrm(ks[8], (N_SSD, CONV_DIM), 0.01),
        "ssd_dt_bias": dt0 + jnp.log(-jnp.expm1(-dt0)),
        "ssd_a_log": jnp.log(jax.random.uniform(ks[10], (N_SSD, 2, SSD_HEADS), f32, 1.0, 16.0)),
        "ssd_d": 1.0 + nrm(ks[11], (N_SSD, SSD_HEADS), 0.02),
        "ssd_norm_w": 1.0 + nrm(ks[12], (N_SSD, D_INNER), 0.02),
        "ssd_w_out": nrm(ks[13], (N_SSD, D_INNER, D_MODEL), D_INNER ** -0.5),
        "ffn_w_gate": nrm(ks[14], (DEPTH, D_MODEL, D_FF), D_MODEL ** -0.5),
        "ffn_w_up": nrm(ks[15], (DEPTH, D_MODEL, D_FF), D_MODEL ** -0.5),
        "ffn_w_down": nrm(ks[16], (DEPTH, D_FF, D_MODEL), D_FF ** -0.5),
    }


def reference(x, meta_tokens, norm_mix_w, norm_ffn_w, norm_final_w, fnet_w_out,
              ssd_w_in, ssd_conv_w, ssd_conv_b, ssd_dt_bias, ssd_a_log, ssd_d,
              ssd_norm_w, ssd_w_out, ffn_w_gate, ffn_w_up, ffn_w_down):
    b = x.shape[0]
    meta = jnp.broadcast_to(meta_tokens.astype(x.dtype)[None], (b, N_META, x.shape[-1]))
    h = jnp.concatenate([meta, x], axis=1)
    for i in range(DEPTH):
        j = i // N_MIXERS
        u = rms_norm(h, norm_mix_w[i])
        if i % N_MIXERS == 0:
            h = h + fourier_mixer(u, fnet_w_out[j])
        else:
            h = h + bidir_ssd_mixer(u, ssd_w_in[j], ssd_conv_w[j], ssd_conv_b[j],
                                    ssd_dt_bias[j], ssd_a_log[j], ssd_d[j],
                                    ssd_norm_w[j], ssd_w_out[j])
        h = h + swiglu_ffn(rms_norm(h, norm_ffn_w[i]), ffn_w_gate[i], ffn_w_up[i], ffn_w_down[i])
    return rms_norm(h, norm_final_w)[:, N_META:]
```

```python
import functools
import math

import jax
import jax.numpy as jnp
from jax import lax
from jax.experimental import pallas as pl
from jax.experimental.pallas import tpu as pltpu

F32 = jnp.float32
BF16 = jnp.bfloat16

N_META = 16
FNET_GROUP_WIDTH = 256
SSD_HEAD_DIM = 64
SSD_GROUPS = 8
SSD_STATE = 128
HEADS_PER_GROUP = 8
GROUP_WIDTH = HEADS_PER_GROUP * SSD_HEAD_DIM
CONV_WIDTH = 5
CHUNK = 256
HALO = 16
EPS = 1e-6
ROW_PAD = 256


def _params(semantics, vmem_mb):
    return pltpu.CompilerParams(dimension_semantics=semantics,
                                vmem_limit_bytes=vmem_mb * 1024 * 1024)


def _rmsnorm_kernel(h_ref, w_ref, o_ref):
    x = h_ref[...]
    ms = jnp.mean(x * x, axis=-1, keepdims=True)
    o_ref[...] = (x * lax.rsqrt(ms + EPS) * w_ref[...]).astype(o_ref.dtype)


def _rmsnorm(h2d, w, out_dtype, tm=512):
    m, d = h2d.shape
    assert m % tm == 0, (m, tm)
    return pl.pallas_call(
        _rmsnorm_kernel,
        grid=(m // tm,),
        in_specs=[pl.BlockSpec((tm, d), lambda i: (i, 0)),
                  pl.BlockSpec((1, d), lambda i: (0, 0))],
        out_specs=pl.BlockSpec((tm, d), lambda i: (i, 0)),
        out_shape=jax.ShapeDtypeStruct((m, d), out_dtype),
        compiler_params=_params(("parallel",), 40),
        name="rmsnorm",
    )(h2d, w.reshape(1, d).astype(F32))


def _mm_kernel(a_ref, w_ref, o_ref):
    o_ref[...] = jnp.dot(a_ref[...], w_ref[...],
                         preferred_element_type=F32).astype(o_ref.dtype)


def _mm_res_kernel(a_ref, w_ref, r_ref, o_ref):
    acc = jnp.dot(a_ref[...], w_ref[...], preferred_element_type=F32)
    o_ref[...] = (r_ref[...] + acc).astype(o_ref.dtype)


def _matmul(a, w, out_dtype, tm, tn, res=None, vmem_mb=48):
    m, k = a.shape
    n = w.shape[1]
    assert m % tm == 0 and n % tn == 0, (m, n, tm, tn)
    in_specs = [pl.BlockSpec((tm, k), lambda i, j: (i, 0)),
                pl.BlockSpec((k, tn), lambda i, j: (0, j))]
    args = [a, w]
    kern = _mm_kernel
    aliases = {}
    if res is not None:
        in_specs.append(pl.BlockSpec((tm, tn), lambda i, j: (i, j)))
        args.append(res)
        kern = _mm_res_kernel
        aliases = {2: 0}
    return pl.pallas_call(
        kern,
        grid=(m // tm, n // tn),
        in_specs=in_specs,
        out_specs=pl.BlockSpec((tm, tn), lambda i, j: (i, j)),
        out_shape=jax.ShapeDtypeStruct((m, n), out_dtype),
        input_output_aliases=aliases,
        compiler_params=_params(("parallel", "parallel"), vmem_mb),
        name="matmul_res" if res is not None else "matmul",
    )(*args)


def _swiglu_kernel(a_ref, wg_ref, wu_ref, o_ref):
    a = a_ref[...]
    gate = jnp.dot(a, wg_ref[...], preferred_element_type=F32)
    up = jnp.dot(a, wu_ref[...], preferred_element_type=F32)
    o_ref[...] = (gate * jax.nn.sigmoid(gate) * up).astype(o_ref.dtype)


def _swiglu(a, wg, wu, tm=1024, tn=512):
    m, k = a.shape
    n = wg.shape[1]
    assert m % tm == 0 and n % tn == 0, (m, n, tm, tn)
    return pl.pallas_call(
        _swiglu_kernel,
        grid=(m // tm, n // tn),
        in_specs=[pl.BlockSpec((tm, k), lambda i, j: (i, 0)),
                  pl.BlockSpec((k, tn), lambda i, j: (0, j)),
                  pl.BlockSpec((k, tn), lambda i, j: (0, j))],
        out_specs=pl.BlockSpec((tm, tn), lambda i, j: (i, j)),
        out_shape=jax.ShapeDtypeStruct((m, n), BF16),
        compiler_params=_params(("parallel", "parallel"), 48),
        name="swiglu",
    )(a, wg, wu)


def _fnet_chan_kernel(h_ref, w_ref, cs_ref, o_ref):
    x = h_ref[...]
    ms = jnp.mean(x * x, axis=-1, keepdims=True)
    u = (x * lax.rsqrt(ms + EPS) * w_ref[...]).astype(BF16)
    gw = FNET_GROUP_WIDTH
    for g in range(u.shape[1] // gw):
        r = jnp.dot(u[:, g * gw:(g + 1) * gw], cs_ref[...], preferred_element_type=F32)
        o_ref[0, :, g * gw:(g + 1) * gw] = r[:, :gw].astype(BF16)
        o_ref[1, :, g * gw:(g + 1) * gw] = r[:, gw:].astype(BF16)


def _fnet_chan(h, w, cs_tab, tm=256):
    b, lp, d = h.shape
    assert lp % tm == 0 and d % FNET_GROUP_WIDTH == 0, (lp, tm, d)
    return pl.pallas_call(
        _fnet_chan_kernel,
        grid=(b, lp // tm),
        in_specs=[pl.BlockSpec((None, tm, d), lambda bi, i: (bi, i, 0)),
                  pl.BlockSpec((1, d), lambda bi, i: (0, 0)),
                  pl.BlockSpec(cs_tab.shape, lambda bi, i: (0, 0))],
        out_specs=pl.BlockSpec((None, 2, tm, d), lambda bi, i: (bi, 0, i, 0)),
        out_shape=jax.ShapeDtypeStruct((b, 2, lp, d), BF16),
        compiler_params=_params(("parallel", "parallel"), 40),
        name="fnet_chan",
    )(h, w.reshape(1, d).astype(F32), cs_tab)


def _seq_dft_kernel(w_ref, ab_ref, o_ref):
    o_ref[...] = jnp.dot(w_ref[...], ab_ref[...],
                         preferred_element_type=F32).astype(o_ref.dtype)


def _seq_dft(wcat, ab, tm=256, tn=512):
    b, k, d = ab.shape
    lp = wcat.shape[0]
    assert lp % tm == 0 and d % tn == 0, (lp, tm, d, tn)
    return pl.pallas_call(
        _seq_dft_kernel,
        grid=(b, d // tn, lp // tm),
        in_specs=[pl.BlockSpec((tm, k), lambda bi, j, i: (i, 0)),
                  pl.BlockSpec((None, k, tn), lambda bi, j, i: (bi, 0, j))],
        out_specs=pl.BlockSpec((None, tm, tn), lambda bi, j, i: (bi, i, j)),
        out_shape=jax.ShapeDtypeStruct((b, lp, d), BF16),
        compiler_params=_params(("parallel", "parallel", "parallel"), 52),
        name="seq_dft",
    )(wcat, ab)


def _chan_dft_table():
    n = FNET_GROUP_WIDTH
    j = jnp.arange(n, dtype=jnp.int32)
    m = (j[:, None] * j[None, :]) % n
    th = m.astype(F32) * (2.0 * math.pi / n)
    scale = 1.0 / math.sqrt(n)
    return (jnp.concatenate([jnp.cos(th), jnp.sin(th)], axis=1) * scale).astype(BF16)


def _seq_dft_table(l_tok, lp):
    blk = 64
    k = jnp.arange(lp, dtype=jnp.int32)[:, None]
    a = jnp.arange(lp // blk, dtype=jnp.int32)[None, :] * blk
    b = jnp.arange(blk, dtype=jnp.int32)[None, :]
    w0 = 2.0 * math.pi / l_tok
    th1 = ((k * a) % l_tok).astype(F32) * w0
    th2 = ((k * b) % l_tok).astype(F32) * w0
    c1, s1 = jnp.cos(th1)[:, :, None], jnp.sin(th1)[:, :, None]
    c2, s2 = jnp.cos(th2)[:, None, :], jnp.sin(th2)[:, None, :]
    valid = (k < l_tok)[:, :, None] & ((a[:, :, None] + b[:, None, :]) < l_tok)
    scale = 1.0 / math.sqrt(l_tok)
    wc = jnp.where(valid, (c1 * c2 - s1 * s2) * scale, 0.0).reshape(lp, lp)
    ws = jnp.where(valid, (s1 * c2 + c1 * s2) * scale, 0.0).reshape(lp, lp)
    return jnp.concatenate([wc, -ws], axis=1).astype(BF16)


def _fourier_layer(h, norm_w, w_out, chan_tab, seq_tab):
    b, lp, d = h.shape
    ab = _fnet_chan(h, norm_w, chan_tab)
    f = _seq_dft(seq_tab, ab.reshape(b, 2 * lp, d))
    h2 = _matmul(f.reshape(b * lp, d), w_out, F32, 1024, 512, res=h.reshape(b * lp, d))
    return h2.reshape(b, lp, d)


def _split3(x):
    hi = x.astype(BF16)
    r = x - hi.astype(F32)
    mid = r.astype(BF16)
    lo = (r - mid.astype(F32)).astype(BF16)
    return hi, mid, lo


def _select_dot(x, sel):
    hi, mid, lo = _split3(x)
    return (jnp.dot(hi, sel, preferred_element_type=F32)
            + jnp.dot(mid, sel, preferred_element_type=F32)
            + jnp.dot(lo, sel, preferred_element_type=F32))


def _head_expand_matrix(first_row, n_rows=128):
    r = lax.broadcasted_iota(jnp.int32, (n_rows, GROUP_WIDTH), 0)
    c = lax.broadcasted_iota(jnp.int32, (n_rows, GROUP_WIDTH), 1)
    return jnp.where(r == first_row + (c >> 6), 1.0, 0.0).astype(BF16)


def _conv_kernel(prev_ref, cur_ref, next_ref, w_ref, b_ref, o_ref, *, l_tok):
    i = pl.program_id(1)
    n = pl.num_programs(1)
    rows = cur_ref.shape[0]
    prev = jnp.where(i > 0, prev_ref[...].astype(F32), 0.0)
    nxt = jnp.where(i < n - 1, next_ref[...].astype(F32), 0.0)
    x = jnp.concatenate([prev, cur_ref[...].astype(F32), nxt], axis=0)
    tot = rows + 2 * HALO
    acc = jnp.zeros((rows, x.shape[1]), F32) + b_ref[...]
    for k in range(CONV_WIDTH):
        shift = (CONV_WIDTH // 2 - k) % tot
        xs = x if shift == 0 else pltpu.roll(x, shift, 0)
        acc = acc + xs[HALO:HALO + rows] * w_ref[k:k + 1, :]
    y = acc * jax.nn.sigmoid(acc)
    row = i * rows + lax.broadcasted_iota(jnp.int32, y.shape, 0)
    o_ref[...] = jnp.where(row < l_tok, y, 0.0).astype(o_ref.dtype)


def _conv_silu(zxbc, conv_w, conv_b, l_tok, d_inner, tc=512):
    b, lp, _ = zxbc.shape
    conv_dim = conv_w.shape[1]
    assert lp % CHUNK == 0 and conv_dim % tc == 0 and d_inner % tc == 0, (lp, conv_dim, d_inner)
    c0 = d_inner // tc
    hb = CHUNK // HALO
    nh = lp // HALO
    return pl.pallas_call(
        functools.partial(_conv_kernel, l_tok=l_tok),
        grid=(b, lp // CHUNK, conv_dim // tc),
        in_specs=[
            pl.BlockSpec((None, HALO, tc), lambda bi, i, j: (bi, jnp.maximum(i * hb - 1, 0), c0 + j)),
            pl.BlockSpec((None, CHUNK, tc), lambda bi, i, j: (bi, i, c0 + j)),
            pl.BlockSpec((None, HALO, tc), lambda bi, i, j: (bi, jnp.minimum((i + 1) * hb, nh - 1), c0 + j)),
            pl.BlockSpec((CONV_WIDTH, tc), lambda bi, i, j: (0, j)),
            pl.BlockSpec((1, tc), lambda bi, i, j: (0, j)),
        ],
        out_specs=pl.BlockSpec((None, CHUNK, tc), lambda bi, i, j: (bi, i, j)),
        out_shape=jax.ShapeDtypeStruct((b, lp, conv_dim), BF16),
        compiler_params=_params(("parallel", "parallel", "parallel"), 32),
        name="conv_silu",
    )(zxbc, zxbc, zxbc, conv_w.astype(F32), conv_b.reshape(1, conv_dim).astype(F32))


def _dt_kernel(raw_ref, bias_ref, alog_ref, dt_ref, cs_ref, colg_ref, rowg_ref, *, l_tok):
    c = pl.program_id(1)
    x = raw_ref[...] + bias_ref[...]
    dt = jnp.maximum(x, 0.0) + jnp.log1p(jnp.exp(-jnp.abs(x)))
    row = c * CHUNK + lax.broadcasted_iota(jnp.int32, x.shape, 0)
    dt = jnp.where(row < l_tok, dt, 0.0)
    da = dt * (-jnp.exp(alog_ref[...]))
    li = lax.broadcasted_iota(jnp.int32, (CHUNK, CHUNK), 0)
    si = lax.broadcasted_iota(jnp.int32, (CHUNK, CHUNK), 1)
    tri_l = jnp.where(li >= si, 1.0, 0.0).astype(BF16)
    tri_u = jnp.where(li <= si, 1.0, 0.0).astype(BF16)
    hi, mid, lo = _split3(da)
    prefix = (jnp.dot(tri_l, hi, preferred_element_type=F32)
              + jnp.dot(tri_l, mid, preferred_element_type=F32)
              + jnp.dot(tri_l, lo, preferred_element_type=F32))
    suffix = (jnp.dot(tri_u, hi, preferred_element_type=F32)
              + jnp.dot(tri_u, mid, preferred_element_type=F32)
              + jnp.dot(tri_u, lo, preferred_element_type=F32))
    col = lax.broadcasted_iota(jnp.int32, x.shape, 1)
    cs = jnp.where((col & HEADS_PER_GROUP) != 0, suffix, prefix)
    dt_ref[...] = dt
    cs_ref[...] = cs
    cs_t = cs.T
    dt_t = dt.T
    w = 2 * HEADS_PER_GROUP
    for g in range(SSD_GROUPS):
        colg_ref[g] = cs[:, g * w:(g + 1) * w]
        rowg_ref[g, 0:w, :] = cs_t[g * w:(g + 1) * w, :]
        rowg_ref[g, w:2 * w, :] = dt_t[g * w:(g + 1) * w, :]


def _dt_prepare(dt_raw, dt_bias, a_log, l_tok):
    b, lp, nh2 = dt_raw.shape
    assert lp % CHUNK == 0 and nh2 == 2 * HEADS_PER_GROUP * SSD_GROUPS, (lp, nh2)
    nc = lp // CHUNK
    w = 2 * HEADS_PER_GROUP
    blk = lambda bi, c: (bi, c, 0)
    return pl.pallas_call(
        functools.partial(_dt_kernel, l_tok=l_tok),
        grid=(b, nc),
        in_specs=[pl.BlockSpec((None, CHUNK, nh2), blk),
                  pl.BlockSpec((1, nh2), lambda bi, c: (0, 0)),
                  pl.BlockSpec((1, nh2), lambda bi, c: (0, 0))],
        out_specs=[pl.BlockSpec((None, CHUNK, nh2), blk),
                   pl.BlockSpec((None, CHUNK, nh2), blk),
                   pl.BlockSpec((None, None, SSD_GROUPS, CHUNK, w), lambda bi, c: (bi, c, 0, 0, 0)),
                   pl.BlockSpec((None, None, SSD_GROUPS, 2 * w, CHUNK), lambda bi, c: (bi, c, 0, 0, 0))],
        out_shape=[jax.ShapeDtypeStruct((b, lp, nh2), F32),
                   jax.ShapeDtypeStruct((b, lp, nh2), F32),
                   jax.ShapeDtypeStruct((b, nc, SSD_GROUPS, CHUNK, w), F32),
                   jax.ShapeDtypeStruct((b, nc, SSD_GROUPS, 2 * w, CHUNK), F32)],
        compiler_params=_params(("parallel", "parallel"), 32),
        name="dt_prepare",
    )(dt_raw, dt_bias.reshape(1, nh2).astype(F32), a_log.reshape(1, nh2).astype(F32))


def _state_kernel(xf_ref, bf_ref, dtf_ref, csf_ref, xb_ref, bb_ref, dtb_ref, csb_ref,
                  hf_out, hb_out, hf, hb):
    t = pl.program_id(1)

    @pl.when(t == 0)
    def _():
        hf[...] = jnp.zeros_like(hf)
        hb[...] = jnp.zeros_like(hb)

    def one_direction(x_ref, b_ref, dt_ref, cs_ref, h_out, h, total_row, dir_off):
        cs = cs_ref[...]
        total = cs[total_row:total_row + 1, :]
        col = lax.broadcasted_iota(jnp.int32, cs.shape, 1)
        mine = (col & HEADS_PER_GROUP) == dir_off
        scale = (dt_ref[...] * jnp.exp(jnp.where(mine, total - cs, 0.0))).astype(BF16)
        decay8 = jnp.broadcast_to(jnp.exp(total), (8, cs.shape[1]))
        for g in range(SSD_GROUPS):
            sel = _head_expand_matrix(g * 2 * HEADS_PER_GROUP + dir_off)
            sc = jnp.dot(scale, sel, preferred_element_type=F32).astype(BF16)
            xdec = x_ref[:, g * GROUP_WIDTH:(g + 1) * GROUP_WIDTH] * sc
            s_new = lax.dot_general(b_ref[:, g * SSD_STATE:(g + 1) * SSD_STATE], xdec,
                                    (((0,), (0,)), ((), ())), preferred_element_type=F32)
            dec = _select_dot(decay8, sel)[0:1, :]
            h_prev = h[g]
            h_out[g] = h_prev.astype(h_out.dtype)
            h[g] = h_prev * dec + s_new

    one_direction(xf_ref, bf_ref, dtf_ref, csf_ref, hf_out, hf, CHUNK - 1, 0)
    one_direction(xb_ref, bb_ref, dtb_ref, csb_ref, hb_out, hb, 0, HEADS_PER_GROUP)


def _ssd_states(xbc, dt, cs, d_inner):
    b, lp, _ = xbc.shape
    nc = lp // CHUNK
    gn = SSD_GROUPS * SSD_STATE
    nh2 = dt.shape[2]
    fwd = lambda bi, t: (bi, t, 0)
    bwd = lambda bi, t: (bi, nc - 1 - t, 0)
    fwd_b = lambda bi, t: (bi, t, d_inner // gn)
    bwd_b = lambda bi, t: (bi, nc - 1 - t, d_inner // gn)
    st_shape = jax.ShapeDtypeStruct((b, nc, SSD_GROUPS, SSD_STATE, GROUP_WIDTH), BF16)
    st_block = (None, None, SSD_GROUPS, SSD_STATE, GROUP_WIDTH)
    return pl.pallas_call(
        _state_kernel,
        grid=(b, nc),
        in_specs=[pl.BlockSpec((None, CHUNK, d_inner), fwd),
                  pl.BlockSpec((None, CHUNK, gn), fwd_b),
                  pl.BlockSpec((None, CHUNK, nh2), fwd),
                  pl.BlockSpec((None, CHUNK, nh2), fwd),
                  pl.BlockSpec((None, CHUNK, d_inner), bwd),
                  pl.BlockSpec((None, CHUNK, gn), bwd_b),
                  pl.BlockSpec((None, CHUNK, nh2), bwd),
                  pl.BlockSpec((None, CHUNK, nh2), bwd)],
        out_specs=[pl.BlockSpec(st_block, lambda bi, t: (bi, t, 0, 0, 0)),
                   pl.BlockSpec(st_block, lambda bi, t: (bi, nc - 1 - t, 0, 0, 0))],
        out_shape=[st_shape, st_shape],
        scratch_shapes=[pltpu.VMEM((SSD_GROUPS, SSD_STATE, GROUP_WIDTH), F32),
                        pltpu.VMEM((SSD_GROUPS, SSD_STATE, GROUP_WIDTH), F32)],
        compiler_params=_params(("parallel", "arbitrary"), 40),
        name="ssd_states",
    )(xbc, xbc, dt, cs, xbc, xbc, dt, cs)


def _ssd_out_kernel(x_ref, b_ref, c_ref, z_ref, cs_ref, colg_ref, rowg_ref, hf_ref, hb_ref,
                    dskip_ref, nw_ref, o_ref):
    g = pl.program_id(2)
    x = x_ref[...]
    cm = c_ref[...]
    cb = lax.dot_general(cm, b_ref[...], (((1,), (1,)), ((), ())), preferred_element_type=F32)
    colg = colg_ref[...]
    rowg = rowg_ref[...]
    hpg = HEADS_PER_GROUP
    li = lax.broadcasted_iota(jnp.int32, (CHUNK, CHUNK), 0)
    si = lax.broadcasted_iota(jnp.int32, (CHUNK, CHUNK), 1)
    lane = lax.broadcasted_iota(jnp.int32, (CHUNK, 2 * SSD_HEAD_DIM), 1)
    zero_bf = jnp.zeros((CHUNK, 2 * SSD_HEAD_DIM), BF16)
    ys = []
    for p in range(hpg // 2):
        gms = []
        for hh in (2 * p, 2 * p + 1):
            cf_col = colg[:, hh:hh + 1]
            sb_col = colg[:, hpg + hh:hpg + hh + 1]
            cf_row = rowg[hh:hh + 1, :]
            sb_row = rowg[hpg + hh:hpg + hh + 1, :]
            dtf_row = rowg[2 * hpg + hh:2 * hpg + hh + 1, :]
            dtb_row = rowg[3 * hpg + hh:3 * hpg + hh + 1, :]
            arg = jnp.where(li >= si, cf_col - cf_row, sb_col - sb_row)
            wgt = jnp.where(li > si, dtf_row, jnp.where(li < si, dtb_row, dtf_row + dtb_row))
            gms.append((cb * jnp.exp(arg) * wgt).astype(BF16))
        g2 = jnp.concatenate(gms, axis=1)
        xp = x[:, p * 2 * SSD_HEAD_DIM:(p + 1) * 2 * SSD_HEAD_DIM]
        xbd = jnp.concatenate([jnp.where(lane < SSD_HEAD_DIM, xp, zero_bf),
                               jnp.where(lane >= SSD_HEAD_DIM, xp, zero_bf)], axis=0)
        ys.append(jnp.dot(g2, xbd, preferred_element_type=F32))
    y = jnp.concatenate(ys, axis=1)

    edec = jnp.exp(cs_ref[...]).astype(BF16)
    ef = jnp.dot(edec, _head_expand_matrix(g * 2 * hpg), preferred_element_type=F32)
    eb = jnp.dot(edec, _head_expand_matrix(g * 2 * hpg + hpg), preferred_element_type=F32)
    y = y + jnp.dot(cm, hf_ref[...], preferred_element_type=F32) * ef
    y = y + jnp.dot(cm, hb_ref[...], preferred_element_type=F32) * eb
    y = y + x.astype(F32) * dskip_ref[...]

    z = z_ref[...].astype(F32)
    gz = y * (z * jax.nn.sigmoid(z))
    ms = jnp.mean(gz * gz, axis=-1, keepdims=True)
    o_ref[...] = (gz * lax.rsqrt(ms + EPS) * nw_ref[...]).astype(o_ref.dtype)


def _ssd_out(zxbc, xbc, cs, colg, rowg, hf, hb, d_skip, norm_w, d_inner):
    b, lp, _ = xbc.shape
    nc = lp // CHUNK
    w = 2 * HEADS_PER_GROUP
    nb0 = d_inner // SSD_STATE
    nc0 = nb0 + SSD_GROUPS
    st_block = (None, None, None, SSD_STATE, GROUP_WIDTH)
    return pl.pallas_call(
        _ssd_out_kernel,
        grid=(b, nc, SSD_GROUPS),
        in_specs=[
            pl.BlockSpec((None, CHUNK, GROUP_WIDTH), lambda bi, c, g: (bi, c, g)),
            pl.BlockSpec((None, CHUNK, SSD_STATE), lambda bi, c, g: (bi, c, nb0 + g)),
            pl.BlockSpec((None, CHUNK, SSD_STATE), lambda bi, c, g: (bi, c, nc0 + g)),
            pl.BlockSpec((None, CHUNK, GROUP_WIDTH), lambda bi, c, g: (bi, c, g)),
            pl.BlockSpec((None, CHUNK, cs.shape[2]), lambda bi, c, g: (bi, c, 0)),
            pl.BlockSpec((None, None, None, CHUNK, w), lambda bi, c, g: (bi, c, g, 0, 0)),
            pl.BlockSpec((None, None, None, 2 * w, CHUNK), lambda bi, c, g: (bi, c, g, 0, 0)),
            pl.BlockSpec(st_block, lambda bi, c, g: (bi, c, g, 0, 0)),
            pl.BlockSpec(st_block, lambda bi, c, g: (bi, c, g, 0, 0)),
            pl.BlockSpec((1, GROUP_WIDTH), lambda bi, c, g: (0, g)),
            pl.BlockSpec((1, GROUP_WIDTH), lambda bi, c, g: (0, g)),
        ],
        out_specs=pl.BlockSpec((None, CHUNK, GROUP_WIDTH), lambda bi, c, g: (bi, c, g)),
        out_shape=jax.ShapeDtypeStruct((b, lp, d_inner), BF16),
        compiler_params=_params(("parallel", "parallel", "parallel"), 32),
        name="ssd_out",
    )(xbc, xbc, xbc, zxbc, cs, colg, rowg, hf, hb,
      d_skip.reshape(1, d_inner).astype(F32), norm_w.reshape(1, d_inner).astype(F32))


def _group_major_dt_perm(n_heads):
    perm = []
    for g in range(SSD_GROUPS):
        for direction in range(2):
            for e in range(HEADS_PER_GROUP):
                perm.append(direction * n_heads + g * HEADS_PER_GROUP + e)
    return jnp.array(perm, dtype=jnp.int32)


def _ssd_layer(h, l_tok, norm_w, w_in, conv_w, conv_b, dt_bias, a_log, d_skip, gnorm_w, w_out):
    b, lp, d = h.shape
    d_inner = w_out.shape[0]
    n_heads = d_inner // SSD_HEAD_DIM
    n_main = d_inner + conv_w.shape[1]
    perm = _group_major_dt_perm(n_heads)
    h2d = h.reshape(b * lp, d)
    u = _rmsnorm(h2d, norm_w, BF16)
    zxbc = _matmul(u, w_in[:, :n_main].astype(BF16), BF16, 1024, 512).reshape(b, lp, n_main)
    w_dt = jnp.take(w_in[:, n_main:], perm, axis=1).astype(BF16)
    dt_raw = _matmul(u, w_dt, F32, 1024, 2 * n_heads).reshape(b, lp, 2 * n_heads)
    xbc = _conv_silu(zxbc, conv_w, conv_b, l_tok, d_inner)
    dt, cs, colg, rowg = _dt_prepare(dt_raw, jnp.take(dt_bias.reshape(-1), perm),
                                     jnp.take(a_log.reshape(-1), perm), l_tok)
    hf, hb = _ssd_states(xbc, dt, cs, d_inner)
    yn = _ssd_out(zxbc, xbc, cs, colg, rowg, hf, hb,
                  jnp.repeat(d_skip, SSD_HEAD_DIM), gnorm_w, d_inner)
    h2 = _matmul(yn.reshape(b * lp, d_inner), w_out.astype(BF16), F32, 512, 512, res=h2d)
    return h2.reshape(b, lp, d)


def _ffn_layer(h, norm_w, w_gate, w_up, w_down):
    b, lp, d = h.shape
    h2d = h.reshape(b * lp, d)
    u = _rmsnorm(h2d, norm_w, BF16)
    act = _swiglu(u, w_gate.astype(BF16), w_up.astype(BF16))
    h2 = _matmul(act, w_down.astype(BF16), F32, 512, 512, res=h2d)
    return h2.reshape(b, lp, d)


def _trunk(x, meta_tokens, norm_mix_w, norm_ffn_w, norm_final_w, fnet_w_out,
           ssd_w_in, ssd_conv_w, ssd_conv_b, ssd_dt_bias, ssd_a_log, ssd_d,
           ssd_norm_w, ssd_w_out, ffn_w_gate, ffn_w_up, ffn_w_down):
    b, seq, d = x.shape
    n_meta = meta_tokens.shape[0]
    l_tok = n_meta + seq
    lp = -(-l_tok // ROW_PAD) * ROW_PAD
    depth = norm_mix_w.shape[0]
    meta = jnp.broadcast_to(meta_tokens.astype(x.dtype)[None], (b, n_meta, d))
    h = jnp.concatenate([meta, x, jnp.zeros((b, lp - l_tok, d), x.dtype)], axis=1)
    chan_tab = _chan_dft_table()
    seq_tab = _seq_dft_table(l_tok, lp)
    for i in range(depth):
        j = i // 2
        if i % 2 == 0:
            h = _fourier_layer(h, norm_mix_w[i], fnet_w_out[j].astype(BF16), chan_tab, seq_tab)
        else:
            h = _ssd_layer(h, l_tok, norm_mix_w[i], ssd_w_in[j], ssd_conv_w[j], ssd_conv_b[j],
                           ssd_dt_bias[j], ssd_a_log[j], ssd_d[j], ssd_norm_w[j], ssd_w_out[j])
        h = _ffn_layer(h, norm_ffn_w[i], ffn_w_gate[i], ffn_w_up[i], ffn_w_down[i])
    out = _rmsnorm(h.reshape(b * lp, d), norm_final_w, x.dtype).reshape(b, lp, d)
    return out[:, n_meta:l_tok]


def kernel(x, meta_tokens, norm_mix_w, norm_ffn_w, norm_final_w, fnet_w_out, ssd_w_in, ssd_conv_w, ssd_conv_b, ssd_dt_bias, ssd_a_log, ssd_d, ssd_norm_w, ssd_w_out, ffn_w_gate, ffn_w_up, ffn_w_down):
    return _trunk(x, meta_tokens, norm_mix_w, norm_ffn_w, norm_final_w, fnet_w_out,
                  ssd_w_in, ssd_conv_w, ssd_conv_b, ssd_dt_bias, ssd_a_log, ssd_d,
                  ssd_norm_w, ssd_w_out, ffn_w_gate, ffn_w_up, ffn_w_down)
```

```python
import functools
import math

import jax
import jax.numpy as jnp
from jax import lax
from jax.experimental import pallas as pl
from jax.experimental.pallas import tpu as pltpu

F32 = jnp.float32
BF16 = jnp.bfloat16

N_META = 16
FNET_GROUP_WIDTH = 256
SSD_HEAD_DIM = 64
SSD_GROUPS = 8
SSD_STATE = 128
HEADS_PER_GROUP = 8
GROUP_WIDTH = HEADS_PER_GROUP * SSD_HEAD_DIM
CONV_WIDTH = 5
CHUNK = 256
HALO = 16
EPS = 1e-6
ROW_PAD = 256


def _params(semantics, vmem_mb):
    return pltpu.CompilerParams(dimension_semantics=semantics,
                                vmem_limit_bytes=vmem_mb * 1024 * 1024)


def _rmsnorm_kernel(h_ref, w_ref, o_ref):
    x = h_ref[...]
    ms = jnp.mean(x * x, axis=-1, keepdims=True)
    o_ref[...] = (x * lax.rsqrt(ms + EPS) * w_ref[...]).astype(o_ref.dtype)


def _rmsnorm(h2d, w, out_dtype, tm=512):
    m, d = h2d.shape
    assert m % tm == 0, (m, tm)
    return pl.pallas_call(
        _rmsnorm_kernel,
        grid=(m // tm,),
        in_specs=[pl.BlockSpec((tm, d), lambda i: (i, 0)),
                  pl.BlockSpec((1, d), lambda i: (0, 0))],
        out_specs=pl.BlockSpec((tm, d), lambda i: (i, 0)),
        out_shape=jax.ShapeDtypeStruct((m, d), out_dtype),
        compiler_params=_params(("parallel",), 40),
        name="rmsnorm",
    )(h2d, w.reshape(1, d).astype(F32))


def _mm_kernel(a_ref, w_ref, *rest, has_res):
    if has_res:
        r_ref, o_ref, wbf = rest
    else:
        o_ref, wbf = rest

    @pl.when(pl.program_id(1) == 0)
    def _():
        wbf[...] = w_ref[...].astype(BF16)

    acc = jnp.dot(a_ref[...], wbf[...], preferred_element_type=F32)
    if has_res:
        acc = r_ref[...] + acc
    o_ref[...] = acc.astype(o_ref.dtype)


def _matmul(a, w_stack, layer, n, out_dtype, tm, tn, col0=0, res=None, w_buffers=2, vmem_mb=48):
    m, k = a.shape
    assert w_stack.shape[1] == k, (w_stack.shape, k)
    assert m % tm == 0 and n % tn == 0 and col0 % tn == 0, (m, n, tm, tn, col0)
    cb = col0 // tn
    w_kwargs = {} if w_buffers == 2 else {"pipeline_mode": pl.Buffered(w_buffers)}
    in_specs = [pl.BlockSpec((tm, k), lambda j, i: (i, 0)),
                pl.BlockSpec((None, k, tn), lambda j, i: (layer, 0, cb + j), **w_kwargs)]
    args = [a, w_stack]
    aliases = {}
    if res is not None:
        in_specs.append(pl.BlockSpec((tm, tn), lambda j, i: (i, j)))
        args.append(res)
        aliases = {2: 0}
    return pl.pallas_call(
        functools.partial(_mm_kernel, has_res=res is not None),
        grid=(n // tn, m // tm),
        in_specs=in_specs,
        out_specs=pl.BlockSpec((tm, tn), lambda j, i: (i, j)),
        out_shape=jax.ShapeDtypeStruct((m, n), out_dtype),
        scratch_shapes=[pltpu.VMEM((k, tn), BF16)],
        input_output_aliases=aliases,
        compiler_params=_params(("parallel", "arbitrary"), vmem_mb),
        name="matmul_res" if res is not None else "matmul",
    )(*args)


def _swiglu_kernel(a_ref, wg_ref, wu_ref, o_ref, wg_bf, wu_bf):
    @pl.when(pl.program_id(1) == 0)
    def _():
        wg_bf[...] = wg_ref[...].astype(BF16)
        wu_bf[...] = wu_ref[...].astype(BF16)

    a = a_ref[...]
    gate = jnp.dot(a, wg_bf[...], preferred_element_type=F32)
    up = jnp.dot(a, wu_bf[...], preferred_element_type=F32)
    o_ref[...] = (gate * jax.nn.sigmoid(gate) * up).astype(o_ref.dtype)


def _swiglu(a, wg_stack, wu_stack, layer, tm=1024, tn=512):
    m, k = a.shape
    n = wg_stack.shape[2]
    assert m % tm == 0 and n % tn == 0, (m, n, tm, tn)
    w_spec = pl.BlockSpec((None, k, tn), lambda j, i: (layer, 0, j))
    return pl.pallas_call(
        _swiglu_kernel,
        grid=(n // tn, m // tm),
        in_specs=[pl.BlockSpec((tm, k), lambda j, i: (i, 0)), w_spec, w_spec],
        out_specs=pl.BlockSpec((tm, tn), lambda j, i: (i, j)),
        out_shape=jax.ShapeDtypeStruct((m, n), BF16),
        scratch_shapes=[pltpu.VMEM((k, tn), BF16), pltpu.VMEM((k, tn), BF16)],
        compiler_params=_params(("parallel", "arbitrary"), 48),
        name="swiglu",
    )(a, wg_stack, wu_stack)


FNET_TM = 256


def _mirror_tile(l_tok, i):
    return (l_tok - (i + 1) * FNET_TM) // FNET_TM


def _mirror_select(l_tok):
    r = lax.broadcasted_iota(jnp.int32, (FNET_TM, 2 * FNET_TM), 0)
    c = lax.broadcasted_iota(jnp.int32, (FNET_TM, 2 * FNET_TM), 1)
    return jnp.where(c == (l_tok % FNET_TM) + FNET_TM - r, 1.0, 0.0).astype(BF16)


def _fnet_chan_kernel(h_ref, ha_ref, hb_ref, w_ref, c_ref, s_ref, o_ref, *, l_tok):
    def norm(ref):
        x = ref[...]
        ms = jnp.mean(x * x, axis=-1, keepdims=True)
        return (x * lax.rsqrt(ms + EPS) * w_ref[...]).astype(BF16)

    u = norm(h_ref).astype(F32)
    mirror_src = jnp.concatenate([norm(ha_ref), norm(hb_ref)], axis=0)
    um = jnp.dot(_mirror_select(l_tok), mirror_src, preferred_element_type=F32)
    us = (u + um).astype(BF16)
    ud = (u - um).astype(BF16)
    gw = FNET_GROUP_WIDTH
    for g in range(us.shape[1] // gw):
        cols = slice(g * gw, (g + 1) * gw)
        o_ref[0, :, cols] = jnp.dot(us[:, cols], c_ref[...], preferred_element_type=F32).astype(BF16)
        o_ref[1, :, cols] = jnp.dot(ud[:, cols], s_ref[...], preferred_element_type=F32).astype(BF16)


def _fnet_chan(h, w, c_tab, s_tab, l_tok, hp):
    b, lp, d = h.shape
    tm = FNET_TM
    nh = hp // tm
    assert lp % tm == 0 and hp % tm == 0 and d % FNET_GROUP_WIDTH == 0, (lp, hp, d)
    assert l_tok % 2 == 0 and lp > l_tok and l_tok >= nh * tm and hp > l_tok // 2, (l_tok, lp, hp)
    blk = (None, tm, d)
    return pl.pallas_call(
        functools.partial(_fnet_chan_kernel, l_tok=l_tok),
        grid=(b, nh),
        in_specs=[pl.BlockSpec(blk, lambda bi, i: (bi, i, 0)),
                  pl.BlockSpec(blk, lambda bi, i: (bi, _mirror_tile(l_tok, i), 0)),
                  pl.BlockSpec(blk, lambda bi, i: (bi, _mirror_tile(l_tok, i) + 1, 0)),
                  pl.BlockSpec((1, d), lambda bi, i: (0, 0)),
                  pl.BlockSpec(c_tab.shape, lambda bi, i: (0, 0)),
                  pl.BlockSpec(s_tab.shape, lambda bi, i: (0, 0))],
        out_specs=pl.BlockSpec((None, 2, tm, d), lambda bi, i: (bi, 0, i, 0)),
        out_shape=jax.ShapeDtypeStruct((b, 2, hp, d), BF16),
        compiler_params=_params(("parallel", "parallel"), 48),
        name="fnet_chan",
    )(h, h, h, w.reshape(1, d).astype(F32), c_tab, s_tab)


def _seq_dft_kernel(wc_ref, ws_ref, us_ref, ud_ref, o_ref):
    p = jnp.dot(wc_ref[...], us_ref[...], preferred_element_type=F32)
    q = jnp.dot(ws_ref[...], ud_ref[...], preferred_element_type=F32)
    o_ref[0] = (p - q).astype(o_ref.dtype)
    o_ref[1] = (p + q).astype(o_ref.dtype)


def _seq_dft(wc, ws, ab, tn=512):
    b, _, hp, d = ab.shape
    tm = next(t for t in (768, 512, 256) if hp % t == 0)
    assert d % tn == 0 and wc.shape == (hp, hp) and ws.shape == (hp, hp), (d, tn, wc.shape)
    return pl.pallas_call(
        _seq_dft_kernel,
        grid=(b, d // tn, hp // tm),
        in_specs=[pl.BlockSpec((tm, hp), lambda bi, j, i: (i, 0)),
                  pl.BlockSpec((tm, hp), lambda bi, j, i: (i, 0)),
                  pl.BlockSpec((None, None, hp, tn), lambda bi, j, i: (bi, 0, 0, j)),
                  pl.BlockSpec((None, None, hp, tn), lambda bi, j, i: (bi, 1, 0, j))],
        out_specs=pl.BlockSpec((None, 2, tm, tn), lambda bi, j, i: (bi, 0, i, j)),
        out_shape=jax.ShapeDtypeStruct((b, 2, hp, d), BF16),
        compiler_params=_params(("parallel", "parallel", "parallel"), 48),
        name="seq_dft",
    )(wc, ws, ab, ab)


def _fnet_out_kernel(fd_ref, fa_ref, fb_ref, w_ref, h_ref, o_ref, wbf, *, l_tok, tiles_per_seq):
    t = pl.program_id(1)

    @pl.when(t == 0)
    def _():
        wbf[...] = w_ref[...].astype(BF16)

    tm = FNET_TM
    mirror_src = jnp.concatenate([fa_ref[...], fb_ref[...]], axis=0)
    fm = jnp.dot(_mirror_select(l_tok), mirror_src, preferred_element_type=F32)
    row = (t % tiles_per_seq) * tm + lax.broadcasted_iota(jnp.int32, (tm, 1), 0)
    f = jnp.where(row <= l_tok // 2, fd_ref[...].astype(F32), fm)
    f = jnp.where(row < l_tok, f, 0.0).astype(BF16)
    o_ref[...] = h_ref[...] + jnp.dot(f, wbf[...], preferred_element_type=F32)


def _fnet_out(fm, w_stack, layer, h2d, l_tok, lp, tn=1024):
    b, _, hp, d = fm.shape
    tm = FNET_TM
    nh = hp // tm
    tps = lp // tm
    n = w_stack.shape[2]
    assert n % tn == 0 and h2d.shape == (b * lp, n), (n, tn, h2d.shape)
    clip = lambda v: jnp.clip(v, 0, nh - 1)
    blk = (None, None, tm, d)
    return pl.pallas_call(
        functools.partial(_fnet_out_kernel, l_tok=l_tok, tiles_per_seq=tps),
        grid=(n // tn, b * tps),
        in_specs=[pl.BlockSpec(blk, lambda j, t: (t // tps, 0, clip(t % tps), 0)),
                  pl.BlockSpec(blk, lambda j, t: (t // tps, 1, clip(_mirror_tile(l_tok, t % tps)), 0)),
                  pl.BlockSpec(blk, lambda j, t: (t // tps, 1, clip(_mirror_tile(l_tok, t % tps) + 1), 0)),
                  pl.BlockSpec((None, d, tn), lambda j, t: (layer, 0, j)),
                  pl.BlockSpec((tm, tn), lambda j, t: (t, j))],
        out_specs=pl.BlockSpec((tm, tn), lambda j, t: (t, j)),
        out_shape=jax.ShapeDtypeStruct(h2d.shape, F32),
        scratch_shapes=[pltpu.VMEM((d, tn), BF16)],
        input_output_aliases={4: 0},
        compiler_params=_params(("parallel", "arbitrary"), 48),
        name="fnet_out",
    )(fm, fm, fm, w_stack, h2d)


def _chan_dft_tables():
    n = FNET_GROUP_WIDTH
    j = jnp.arange(n, dtype=jnp.int32)
    th = ((j[:, None] * j[None, :]) % n).astype(F32) * (2.0 * math.pi / n)
    scale = 1.0 / math.sqrt(n)
    return (jnp.cos(th) * scale).astype(BF16), (jnp.sin(th) * scale).astype(BF16)


def _seq_dft_tables(l_tok, hp):
    blk = 64
    half = l_tok // 2
    k = jnp.arange(hp, dtype=jnp.int32)[:, None]
    a = jnp.arange(hp // blk, dtype=jnp.int32)[None, :] * blk
    b = jnp.arange(blk, dtype=jnp.int32)[None, :]
    w0 = 2.0 * math.pi / l_tok
    th1 = ((k * a) % l_tok).astype(F32) * w0
    th2 = ((k * b) % l_tok).astype(F32) * w0
    c1, s1 = jnp.cos(th1)[:, :, None], jnp.sin(th1)[:, :, None]
    c2, s2 = jnp.cos(th2)[:, None, :], jnp.sin(th2)[:, None, :]
    n = a[:, :, None] + b[:, None, :]
    valid = (k <= half)[:, :, None] & (n <= half)
    scale = 1.0 / math.sqrt(l_tok)
    col_w = jnp.where(n == half, 0.5 * scale, scale)
    wc = jnp.where(valid, (c1 * c2 - s1 * s2) * col_w, 0.0).reshape(hp, hp)
    ws = jnp.where(valid, (s1 * c2 + c1 * s2) * scale, 0.0).reshape(hp, hp)
    return wc.astype(BF16), ws.astype(BF16)


def _fourier_layer(h, l_tok, norm_w, w_out_stack, layer, chan_tabs, seq_tabs):
    b, lp, d = h.shape
    hp = seq_tabs[0].shape[0]
    ab = _fnet_chan(h, norm_w, chan_tabs[0], chan_tabs[1], l_tok, hp)
    fm = _seq_dft(seq_tabs[0], seq_tabs[1], ab)
    h2 = _fnet_out(fm, w_out_stack, layer, h.reshape(b * lp, d), l_tok, lp)
    return h2.reshape(b, lp, d)


def _split3(x):
    hi = x.astype(BF16)
    r = x - hi.astype(F32)
    mid = r.astype(BF16)
    lo = (r - mid.astype(F32)).astype(BF16)
    return hi, mid, lo


def _select_dot(x, sel):
    hi, mid, lo = _split3(x)
    return (jnp.dot(hi, sel, preferred_element_type=F32)
            + jnp.dot(mid, sel, preferred_element_type=F32)
            + jnp.dot(lo, sel, preferred_element_type=F32))


def _head_expand_matrix(first_row, n_rows=128):
    r = lax.broadcasted_iota(jnp.int32, (n_rows, GROUP_WIDTH), 0)
    c = lax.broadcasted_iota(jnp.int32, (n_rows, GROUP_WIDTH), 1)
    return jnp.where(r == first_row + (c >> 6), 1.0, 0.0).astype(BF16)


CONV_SUB = 512
SUBLANES = 8


def _conv_kernel(prev_ref, cur_ref, next_ref, w_ref, b_ref, o_ref, xs, *, l_tok):
    i = pl.program_id(1)
    n = pl.num_programs(1)
    rows = cur_ref.shape[0]
    lo = SUBLANES - CONV_WIDTH // 2
    row = i * rows + lax.broadcasted_iota(jnp.int32, (rows, CONV_SUB), 0)
    for s in range(cur_ref.shape[1] // CONV_SUB):
        cols = slice(s * CONV_SUB, (s + 1) * CONV_SUB)
        prev = prev_ref[:, cols].astype(F32)[HALO - SUBLANES:, :]
        nxt = next_ref[:, cols].astype(F32)[:SUBLANES, :]
        xs[0:SUBLANES, cols] = jnp.where(i > 0, prev, 0.0)
        xs[SUBLANES:SUBLANES + rows, cols] = cur_ref[:, cols].astype(F32)
        xs[SUBLANES + rows:2 * SUBLANES + rows, cols] = jnp.where(i < n - 1, nxt, 0.0)
        acc = b_ref[:, cols] + xs[lo:lo + rows, cols] * w_ref[0:1, cols]
        for k in range(1, CONV_WIDTH):
            acc = acc + xs[lo + k:lo + k + rows, cols] * w_ref[k:k + 1, cols]
        y = acc * jax.nn.sigmoid(acc)
        o_ref[:, cols] = jnp.where(row < l_tok, y, 0.0).astype(o_ref.dtype)


def _conv_silu(zxbc, conv_w, conv_b, l_tok, d_inner, tc=2048):
    b, lp, _ = zxbc.shape
    conv_dim = conv_w.shape[1]
    assert lp % CHUNK == 0 and conv_dim % tc == 0 and d_inner % tc == 0, (lp, conv_dim, d_inner)
    assert tc % CONV_SUB == 0 and HALO >= SUBLANES >= CONV_WIDTH // 2
    c0 = d_inner // tc
    hb = CHUNK // HALO
    nh = lp // HALO
    return pl.pallas_call(
        functools.partial(_conv_kernel, l_tok=l_tok),
        grid=(b, lp // CHUNK, conv_dim // tc),
        in_specs=[
            pl.BlockSpec((None, HALO, tc), lambda bi, i, j: (bi, jnp.maximum(i * hb - 1, 0), c0 + j)),
            pl.BlockSpec((None, CHUNK, tc), lambda bi, i, j: (bi, i, c0 + j)),
            pl.BlockSpec((None, HALO, tc), lambda bi, i, j: (bi, jnp.minimum((i + 1) * hb, nh - 1), c0 + j)),
            pl.BlockSpec((CONV_WIDTH, tc), lambda bi, i, j: (0, j)),
            pl.BlockSpec((1, tc), lambda bi, i, j: (0, j)),
        ],
        out_specs=pl.BlockSpec((None, CHUNK, tc), lambda bi, i, j: (bi, i, j)),
        out_shape=jax.ShapeDtypeStruct((b, lp, conv_dim), BF16),
        scratch_shapes=[pltpu.VMEM((CHUNK + 2 * SUBLANES, tc), F32)],
        compiler_params=_params(("parallel", "parallel", "parallel"), 32),
        name="conv_silu",
    )(zxbc, zxbc, zxbc, conv_w.astype(F32), conv_b.reshape(1, conv_dim).astype(F32))


def _dt_kernel(raw_ref, bias_ref, alog_ref, dt_ref, cs_ref, colg_ref, rowg_ref, *, l_tok):
    c = pl.program_id(1)
    x = raw_ref[...] + bias_ref[...]
    dt = jnp.maximum(x, 0.0) + jnp.log1p(jnp.exp(-jnp.abs(x)))
    row = c * CHUNK + lax.broadcasted_iota(jnp.int32, x.shape, 0)
    dt = jnp.where(row < l_tok, dt, 0.0)
    da = dt * (-jnp.exp(alog_ref[...]))
    li = lax.broadcasted_iota(jnp.int32, (CHUNK, CHUNK), 0)
    si = lax.broadcasted_iota(jnp.int32, (CHUNK, CHUNK), 1)
    tri_l = jnp.where(li >= si, 1.0, 0.0).astype(BF16)
    tri_u = jnp.where(li <= si, 1.0, 0.0).astype(BF16)
    hi, mid, lo = _split3(da)
    prefix = (jnp.dot(tri_l, hi, preferred_element_type=F32)
              + jnp.dot(tri_l, mid, preferred_element_type=F32)
              + jnp.dot(tri_l, lo, preferred_element_type=F32))
    suffix = (jnp.dot(tri_u, hi, preferred_element_type=F32)
              + jnp.dot(tri_u, mid, preferred_element_type=F32)
              + jnp.dot(tri_u, lo, preferred_element_type=F32))
    col = lax.broadcasted_iota(jnp.int32, x.shape, 1)
    cs = jnp.where((col & HEADS_PER_GROUP) != 0, suffix, prefix)
    dt_ref[...] = dt
    cs_ref[...] = cs
    cs_t = cs.T
    dt_t = dt.T
    w = 2 * HEADS_PER_GROUP
    for g in range(SSD_GROUPS):
        colg_ref[g] = cs[:, g * w:(g + 1) * w]
        rowg_ref[g, 0:w, :] = cs_t[g * w:(g + 1) * w, :]
        rowg_ref[g, w:2 * w, :] = dt_t[g * w:(g + 1) * w, :]


def _dt_prepare(dt_raw, dt_bias, a_log, l_tok):
    b, lp, nh2 = dt_raw.shape
    assert lp % CHUNK == 0 and nh2 == 2 * HEADS_PER_GROUP * SSD_GROUPS, (lp, nh2)
    nc = lp // CHUNK
    w = 2 * HEADS_PER_GROUP
    blk = lambda bi, c: (bi, c, 0)
    return pl.pallas_call(
        functools.partial(_dt_kernel, l_tok=l_tok),
        grid=(b, nc),
        in_specs=[pl.BlockSpec((None, CHUNK, nh2), blk),
                  pl.BlockSpec((1, nh2), lambda bi, c: (0, 0)),
                  pl.BlockSpec((1, nh2), lambda bi, c: (0, 0))],
        out_specs=[pl.BlockSpec((None, CHUNK, nh2), blk),
                   pl.BlockSpec((None, CHUNK, nh2), blk),
                   pl.BlockSpec((None, None, SSD_GROUPS, CHUNK, w), lambda bi, c: (bi, c, 0, 0, 0)),
                   pl.BlockSpec((None, None, SSD_GROUPS, 2 * w, CHUNK), lambda bi, c: (bi, c, 0, 0, 0))],
        out_shape=[jax.ShapeDtypeStruct((b, lp, nh2), F32),
                   jax.ShapeDtypeStruct((b, lp, nh2), F32),
                   jax.ShapeDtypeStruct((b, nc, SSD_GROUPS, CHUNK, w), F32),
                   jax.ShapeDtypeStruct((b, nc, SSD_GROUPS, 2 * w, CHUNK), F32)],
        compiler_params=_params(("parallel", "parallel"), 32),
        name="dt_prepare",
    )(dt_raw, dt_bias.reshape(1, nh2).astype(F32), a_log.reshape(1, nh2).astype(F32))


def _state_kernel(xf_ref, bf_ref, dtf_ref, csf_ref, xb_ref, bb_ref, dtb_ref, csb_ref,
                  hf_out, hb_out, hf, hb):
    t = pl.program_id(1)

    @pl.when(t == 0)
    def _():
        hf[...] = jnp.zeros_like(hf)
        hb[...] = jnp.zeros_like(hb)

    def one_direction(x_ref, b_ref, dt_ref, cs_ref, h_out, h, total_row, dir_off):
        cs = cs_ref[...]
        total = cs[total_row:total_row + 1, :]
        col = lax.broadcasted_iota(jnp.int32, cs.shape, 1)
        mine = (col & HEADS_PER_GROUP) == dir_off
        scale = (dt_ref[...] * jnp.exp(jnp.where(mine, total - cs, 0.0))).astype(BF16)
        decay8 = jnp.broadcast_to(jnp.exp(total), (8, cs.shape[1]))
        for g in range(SSD_GROUPS):
            sel = _head_expand_matrix(g * 2 * HEADS_PER_GROUP + dir_off)
            sc = jnp.dot(scale, sel, preferred_element_type=F32).astype(BF16)
            xdec = x_ref[:, g * GROUP_WIDTH:(g + 1) * GROUP_WIDTH] * sc
            s_new = lax.dot_general(b_ref[:, g * SSD_STATE:(g + 1) * SSD_STATE], xdec,
                                    (((0,), (0,)), ((), ())), preferred_element_type=F32)
            dec = _select_dot(decay8, sel)[0:1, :]
            h_prev = h[g]
            h_out[g] = h_prev.astype(h_out.dtype)
            h[g] = h_prev * dec + s_new

    one_direction(xf_ref, bf_ref, dtf_ref, csf_ref, hf_out, hf, CHUNK - 1, 0)
    one_direction(xb_ref, bb_ref, dtb_ref, csb_ref, hb_out, hb, 0, HEADS_PER_GROUP)


def _ssd_states(xbc, dt, cs, d_inner):
    b, lp, _ = xbc.shape
    nc = lp // CHUNK
    gn = SSD_GROUPS * SSD_STATE
    nh2 = dt.shape[2]
    fwd = lambda bi, t: (bi, t, 0)
    bwd = lambda bi, t: (bi, nc - 1 - t, 0)
    fwd_b = lambda bi, t: (bi, t, d_inner // gn)
    bwd_b = lambda bi, t: (bi, nc - 1 - t, d_inner // gn)
    st_shape = jax.ShapeDtypeStruct((b, nc, SSD_GROUPS, SSD_STATE, GROUP_WIDTH), BF16)
    st_block = (None, None, SSD_GROUPS, SSD_STATE, GROUP_WIDTH)
    return pl.pallas_call(
        _state_kernel,
        grid=(b, nc),
        in_specs=[pl.BlockSpec((None, CHUNK, d_inner), fwd),
                  pl.BlockSpec((None, CHUNK, gn), fwd_b),
                  pl.BlockSpec((None, CHUNK, nh2), fwd),
                  pl.BlockSpec((None, CHUNK, nh2), fwd),
                  pl.BlockSpec((None, CHUNK, d_inner), bwd),
                  pl.BlockSpec((None, CHUNK, gn), bwd_b),
                  pl.BlockSpec((None, CHUNK, nh2), bwd),
                  pl.BlockSpec((None, CHUNK, nh2), bwd)],
        out_specs=[pl.BlockSpec(st_block, lambda bi, t: (bi, t, 0, 0, 0)),
                   pl.BlockSpec(st_block, lambda bi, t: (bi, nc - 1 - t, 0, 0, 0))],
        out_shape=[st_shape, st_shape],
        scratch_shapes=[pltpu.VMEM((SSD_GROUPS, SSD_STATE, GROUP_WIDTH), F32),
                        pltpu.VMEM((SSD_GROUPS, SSD_STATE, GROUP_WIDTH), F32)],
        compiler_params=_params(("parallel", "arbitrary"), 40),
        name="ssd_states",
    )(xbc, xbc, dt, cs, xbc, xbc, dt, cs)


def _ssd_out_kernel(x_ref, b_ref, c_ref, z_ref, cs_ref, colg_ref, rowg_ref, hf_ref, hb_ref,
                    dskip_ref, nw_ref, o_ref):
    g = pl.program_id(2)
    x = x_ref[...]
    cm = c_ref[...]
    cb = lax.dot_general(cm, b_ref[...], (((1,), (1,)), ((), ())), preferred_element_type=F32)
    colg = colg_ref[...]
    rowg = rowg_ref[...]
    hpg = HEADS_PER_GROUP
    li = lax.broadcasted_iota(jnp.int32, (CHUNK, CHUNK), 0)
    si = lax.broadcasted_iota(jnp.int32, (CHUNK, CHUNK), 1)
    lane = lax.broadcasted_iota(jnp.int32, (CHUNK, 2 * SSD_HEAD_DIM), 1)
    zero_bf = jnp.zeros((CHUNK, 2 * SSD_HEAD_DIM), BF16)
    ys = []
    for p in range(hpg // 2):
        gms = []
        for hh in (2 * p, 2 * p + 1):
            cf_col = colg[:, hh:hh + 1]
            sb_col = colg[:, hpg + hh:hpg + hh + 1]
            cf_row = rowg[hh:hh + 1, :]
            sb_row = rowg[hpg + hh:hpg + hh + 1, :]
            dtf_row = rowg[2 * hpg + hh:2 * hpg + hh + 1, :]
            dtb_row = rowg[3 * hpg + hh:3 * hpg + hh + 1, :]
            arg = jnp.where(li >= si, cf_col - cf_row, sb_col - sb_row)
            wgt = jnp.where(li > si, dtf_row, jnp.where(li < si, dtb_row, dtf_row + dtb_row))
            gms.append((cb * jnp.exp(arg) * wgt).astype(BF16))
        g2 = jnp.concatenate(gms, axis=1)
        xp = x[:, p * 2 * SSD_HEAD_DIM:(p + 1) * 2 * SSD_HEAD_DIM]
        xbd = jnp.concatenate([jnp.where(lane < SSD_HEAD_DIM, xp, zero_bf),
                               jnp.where(lane >= SSD_HEAD_DIM, xp, zero_bf)], axis=0)
        ys.append(jnp.dot(g2, xbd, preferred_element_type=F32))
    y = jnp.concatenate(ys, axis=1)

    edec = jnp.exp(cs_ref[...]).astype(BF16)
    ef = jnp.dot(edec, _head_expand_matrix(g * 2 * hpg), preferred_element_type=F32)
    eb = jnp.dot(edec, _head_expand_matrix(g * 2 * hpg + hpg), preferred_element_type=F32)
    y = y + jnp.dot(cm, hf_ref[...], preferred_element_type=F32) * ef
    y = y + jnp.dot(cm, hb_ref[...], preferred_element_type=F32) * eb
    y = y + x.astype(F32) * dskip_ref[...]

    z = z_ref[...].astype(F32)
    gz = y * (z * jax.nn.sigmoid(z))
    ms = jnp.mean(gz * gz, axis=-1, keepdims=True)
    o_ref[...] = (gz * lax.rsqrt(ms + EPS) * nw_ref[...]).astype(o_ref.dtype)


def _ssd_out(zxbc, xbc, cs, colg, rowg, hf, hb, d_skip, norm_w, d_inner):
    b, lp, _ = xbc.shape
    nc = lp // CHUNK
    w = 2 * HEADS_PER_GROUP
    nb0 = d_inner // SSD_STATE
    nc0 = nb0 + SSD_GROUPS
    st_block = (None, None, None, SSD_STATE, GROUP_WIDTH)
    return pl.pallas_call(
        _ssd_out_kernel,
        grid=(b, nc, SSD_GROUPS),
        in_specs=[
            pl.BlockSpec((None, CHUNK, GROUP_WIDTH), lambda bi, c, g: (bi, c, g)),
            pl.BlockSpec((None, CHUNK, SSD_STATE), lambda bi, c, g: (bi, c, nb0 + g)),
            pl.BlockSpec((None, CHUNK, SSD_STATE), lambda bi, c, g: (bi, c, nc0 + g)),
            pl.BlockSpec((None, CHUNK, GROUP_WIDTH), lambda bi, c, g: (bi, c, g)),
            pl.BlockSpec((None, CHUNK, cs.shape[2]), lambda bi, c, g: (bi, c, 0)),
            pl.BlockSpec((None, None, None, CHUNK, w), lambda bi, c, g: (bi, c, g, 0, 0)),
            pl.BlockSpec((None, None, None, 2 * w, CHUNK), lambda bi, c, g: (bi, c, g, 0, 0)),
            pl.BlockSpec(st_block, lambda bi, c, g: (bi, c, g, 0, 0)),
            pl.BlockSpec(st_block, lambda bi, c, g: (bi, c, g, 0, 0)),
            pl.BlockSpec((1, GROUP_WIDTH), lambda bi, c, g: (0, g)),
            pl.BlockSpec((1, GROUP_WIDTH), lambda bi, c, g: (0, g)),
        ],
        out_specs=pl.BlockSpec((None, CHUNK, GROUP_WIDTH), lambda bi, c, g: (bi, c, g)),
        out_shape=jax.ShapeDtypeStruct((b, lp, d_inner), BF16),
        compiler_params=_params(("parallel", "parallel", "parallel"), 32),
        name="ssd_out",
    )(xbc, xbc, xbc, zxbc, cs, colg, rowg, hf, hb,
      d_skip.reshape(1, d_inner).astype(F32), norm_w.reshape(1, d_inner).astype(F32))


def _group_major_dt_perm(n_heads):
    perm = []
    for g in range(SSD_GROUPS):
        for direction in range(2):
            for e in range(HEADS_PER_GROUP):
                perm.append(direction * n_heads + g * HEADS_PER_GROUP + e)
    return jnp.array(perm, dtype=jnp.int32)


def _ssd_layer(h, l_tok, layer, norm_w, w_in_stack, conv_w, conv_b, dt_bias, a_log, d_skip,
               gnorm_w, w_out_stack):
    b, lp, d = h.shape
    d_inner = w_out_stack.shape[1]
    n_heads = d_inner // SSD_HEAD_DIM
    n_main = d_inner + conv_w.shape[1]
    perm = _group_major_dt_perm(n_heads)
    h2d = h.reshape(b * lp, d)
    u = _rmsnorm(h2d, norm_w, BF16)
    zxbc = _matmul(u, w_in_stack, layer, n_main, BF16, 1024, 512).reshape(b, lp, n_main)
    w_dt = jnp.take(w_in_stack[layer, :, n_main:], perm, axis=1)[None]
    dt_raw = _matmul(u, w_dt, 0, 2 * n_heads, F32, 1024, 2 * n_heads).reshape(b, lp, 2 * n_heads)
    xbc = _conv_silu(zxbc, conv_w, conv_b, l_tok, d_inner)
    dt, cs, colg, rowg = _dt_prepare(dt_raw, jnp.take(dt_bias.reshape(-1), perm),
                                     jnp.take(a_log.reshape(-1), perm), l_tok)
    hf, hb = _ssd_states(xbc, dt, cs, d_inner)
    yn = _ssd_out(zxbc, xbc, cs, colg, rowg, hf, hb,
                  jnp.repeat(d_skip, SSD_HEAD_DIM), gnorm_w, d_inner)
    h2 = _matmul(yn.reshape(b * lp, d_inner), w_out_stack, layer, d, F32, 512, 512, res=h2d)
    return h2.reshape(b, lp, d)


def _ffn_layer(h, layer, norm_w, w_gate_stack, w_up_stack, w_down_stack):
    b, lp, d = h.shape
    h2d = h.reshape(b * lp, d)
    u = _rmsnorm(h2d, norm_w, BF16)
    act = _swiglu(u, w_gate_stack, w_up_stack, layer)
    h2 = _matmul(act, w_down_stack, layer, d, F32, 512, 512, res=h2d, w_buffers=1, vmem_mb=52)
    return h2.reshape(b, lp, d)


def _trunk(x, meta_tokens, norm_mix_w, norm_ffn_w, norm_final_w, fnet_w_out,
           ssd_w_in, ssd_conv_w, ssd_conv_b, ssd_dt_bias, ssd_a_log, ssd_d,
           ssd_norm_w, ssd_w_out, ffn_w_gate, ffn_w_up, ffn_w_down):
    b, seq, d = x.shape
    n_meta = meta_tokens.shape[0]
    l_tok = n_meta + seq
    lp = -(-l_tok // ROW_PAD) * ROW_PAD
    depth = norm_mix_w.shape[0]
    meta = jnp.broadcast_to(meta_tokens.astype(x.dtype)[None], (b, n_meta, d))
    h = jnp.concatenate([meta, x, jnp.zeros((b, lp - l_tok, d), x.dtype)], axis=1)
    hp = -(-(l_tok // 2 + 1) // FNET_TM) * FNET_TM
    chan_tabs = _chan_dft_tables()
    seq_tabs = _seq_dft_tables(l_tok, hp)
    for i in range(depth):
        j = i // 2
        if i % 2 == 0:
            h = _fourier_layer(h, l_tok, norm_mix_w[i], fnet_w_out, j, chan_tabs, seq_tabs)
        else:
            h = _ssd_layer(h, l_tok, j, norm_mix_w[i], ssd_w_in, ssd_conv_w[j], ssd_conv_b[j],
                           ssd_dt_bias[j], ssd_a_log[j], ssd_d[j], ssd_norm_w[j], ssd_w_out)
        h = _ffn_layer(h, i, norm_ffn_w[i], ffn_w_gate, ffn_w_up, ffn_w_down)
    out = _rmsnorm(h.reshape(b * lp, d), norm_final_w, x.dtype).reshape(b, lp, d)
    return out[:, n_meta:l_tok]


def kernel(x, meta_tokens, norm_mix_w, norm_ffn_w, norm_final_w, fnet_w_out, ssd_w_in, ssd_conv_w, ssd_conv_b, ssd_dt_bias, ssd_a_log, ssd_d, ssd_norm_w, ssd_w_out, ffn_w_gate, ffn_w_up, ffn_w_down):
    return _trunk(x, meta_tokens, norm_mix_w, norm_ffn_w, norm_final_w, fnet_w_out,
                  ssd_w_in, ssd_conv_w, ssd_conv_b, ssd_dt_bias, ssd_a_log, ssd_d,
                  ssd_norm_w, ssd_w_out, ffn_w_gate, ffn_w_up, ffn_w_down)
```

```python
import functools
import math

import jax
import jax.numpy as jnp
from jax import lax
from jax.experimental import pallas as pl
from jax.experimental.pallas import tpu as pltpu

F32 = jnp.float32
BF16 = jnp.bfloat16

N_META = 16
FNET_GROUP_WIDTH = 256
SSD_HEAD_DIM = 64
SSD_GROUPS = 8
SSD_STATE = 128
HEADS_PER_GROUP = 8
GROUP_WIDTH = HEADS_PER_GROUP * SSD_HEAD_DIM
CONV_WIDTH = 5
CHUNK = 256
LOG2E = 1.4426950408889634
HALO = 16
EPS = 1e-6
ROW_PAD = 256


def _params(semantics, vmem_mb):
    return pltpu.CompilerParams(dimension_semantics=semantics,
                                vmem_limit_bytes=vmem_mb * 1024 * 1024)


def _rmsnorm_kernel(h_ref, w_ref, o_ref):
    x = h_ref[...]
    ms = jnp.mean(x * x, axis=-1, keepdims=True)
    o_ref[...] = (x * lax.rsqrt(ms + EPS) * w_ref[...]).astype(o_ref.dtype)


def _rmsnorm(h2d, w, out_dtype, tm=512):
    m, d = h2d.shape
    assert m % tm == 0, (m, tm)
    return pl.pallas_call(
        _rmsnorm_kernel,
        grid=(m // tm,),
        in_specs=[pl.BlockSpec((tm, d), lambda i: (i, 0)),
                  pl.BlockSpec((1, d), lambda i: (0, 0))],
        out_specs=pl.BlockSpec((tm, d), lambda i: (i, 0)),
        out_shape=jax.ShapeDtypeStruct((m, d), out_dtype),
        compiler_params=_params(("parallel",), 40),
        name="rmsnorm",
    )(h2d, w.reshape(1, d).astype(F32))


def _final_norm_kernel(h_ref, nxt_ref, w_ref, o_ref, *, shift):
    def norm(x):
        ms = jnp.mean(x * x, axis=-1, keepdims=True)
        return x * lax.rsqrt(ms + EPS) * w_ref[...]

    tm = o_ref.shape[0]
    o_ref[0:tm - shift, :] = norm(h_ref[shift:, :]).astype(o_ref.dtype)
    o_ref[tm - shift:, :] = norm(nxt_ref[...]).astype(o_ref.dtype)


def _final_norm(h, w, n_skip, seq, tm=256):
    b, lp, d = h.shape
    assert seq % tm == 0 and tm % n_skip == 0 and n_skip % SUBLANES == 0 and lp >= seq + tm, (seq, n_skip, lp)
    per = tm // n_skip
    return pl.pallas_call(
        functools.partial(_final_norm_kernel, shift=n_skip),
        grid=(b, seq // tm),
        in_specs=[pl.BlockSpec((None, tm, d), lambda bi, i: (bi, i, 0)),
                  pl.BlockSpec((None, n_skip, d), lambda bi, i: (bi, (i + 1) * per, 0)),
                  pl.BlockSpec((1, d), lambda bi, i: (0, 0))],
        out_specs=pl.BlockSpec((None, tm, d), lambda bi, i: (bi, i, 0)),
        out_shape=jax.ShapeDtypeStruct((b, seq, d), h.dtype),
        compiler_params=_params(("parallel", "parallel"), 40),
        name="final_norm",
    )(h, h, w.reshape(1, d).astype(F32))


def _mm_kernel(a_ref, w_ref, *rest, has_res):
    if has_res:
        r_ref, o_ref, wbf = rest
    else:
        o_ref, wbf = rest

    @pl.when(pl.program_id(1) == 0)
    def _():
        wbf[...] = w_ref[...].astype(BF16)

    acc = jnp.dot(a_ref[...], wbf[...], preferred_element_type=F32)
    if has_res:
        acc = r_ref[...] + acc
    o_ref[...] = acc.astype(o_ref.dtype)


def _matmul(a, w_stack, layer, n, out_dtype, tm, tn, col0=0, res=None, w_buffers=2, vmem_mb=48):
    m, k = a.shape
    assert w_stack.shape[1] == k, (w_stack.shape, k)
    assert m % tm == 0 and n % tn == 0 and col0 % tn == 0, (m, n, tm, tn, col0)
    cb = col0 // tn
    w_kwargs = {} if w_buffers == 2 else {"pipeline_mode": pl.Buffered(w_buffers)}
    in_specs = [pl.BlockSpec((tm, k), lambda j, i: (i, 0)),
                pl.BlockSpec((None, k, tn), lambda j, i: (layer, 0, cb + j), **w_kwargs)]
    args = [a, w_stack]
    aliases = {}
    if res is not None:
        in_specs.append(pl.BlockSpec((tm, tn), lambda j, i: (i, j)))
        args.append(res)
        aliases = {2: 0}
    return pl.pallas_call(
        functools.partial(_mm_kernel, has_res=res is not None),
        grid=(n // tn, m // tm),
        in_specs=in_specs,
        out_specs=pl.BlockSpec((tm, tn), lambda j, i: (i, j)),
        out_shape=jax.ShapeDtypeStruct((m, n), out_dtype),
        scratch_shapes=[pltpu.VMEM((k, tn), BF16)],
        input_output_aliases=aliases,
        compiler_params=_params(("parallel", "arbitrary"), vmem_mb),
        name="matmul_res" if res is not None else "matmul",
    )(*args)


def _swiglu_kernel(a_ref, wg_ref, wu_ref, o_ref, wg_bf, wu_bf):
    @pl.when(pl.program_id(1) == 0)
    def _():
        wg_bf[...] = wg_ref[...].astype(BF16)
        wu_bf[...] = wu_ref[...].astype(BF16)

    a = a_ref[...]
    gate = jnp.dot(a, wg_bf[...], preferred_element_type=F32)
    up = jnp.dot(a, wu_bf[...], preferred_element_type=F32)
    o_ref[...] = (gate * jax.nn.sigmoid(gate) * up).astype(o_ref.dtype)


def _swiglu(a, wg_stack, wu_stack, layer, tm=1024, tn=512):
    m, k = a.shape
    n = wg_stack.shape[2]
    assert m % tm == 0 and n % tn == 0, (m, n, tm, tn)
    w_spec = pl.BlockSpec((None, k, tn), lambda j, i: (layer, 0, j))
    return pl.pallas_call(
        _swiglu_kernel,
        grid=(n // tn, m // tm),
        in_specs=[pl.BlockSpec((tm, k), lambda j, i: (i, 0)), w_spec, w_spec],
        out_specs=pl.BlockSpec((tm, tn), lambda j, i: (i, j)),
        out_shape=jax.ShapeDtypeStruct((m, n), BF16),
        scratch_shapes=[pltpu.VMEM((k, tn), BF16), pltpu.VMEM((k, tn), BF16)],
        compiler_params=_params(("parallel", "arbitrary"), 48),
        name="swiglu",
    )(a, wg_stack, wu_stack)


FNET_TM = 256


def _mirror_tile(l_tok, i):
    return (l_tok - (i + 1) * FNET_TM) // FNET_TM


def _mirror_select(l_tok):
    r = lax.broadcasted_iota(jnp.int32, (FNET_TM, 2 * FNET_TM), 0)
    c = lax.broadcasted_iota(jnp.int32, (FNET_TM, 2 * FNET_TM), 1)
    return jnp.where(c == (l_tok % FNET_TM) + FNET_TM - r, 1.0, 0.0).astype(BF16)


def _fnet_chan_kernel(h_ref, ha_ref, hb_ref, w_ref, c_ref, s_ref, o_ref, *, l_tok):
    def norm(ref):
        x = ref[...]
        ms = jnp.mean(x * x, axis=-1, keepdims=True)
        return (x * lax.rsqrt(ms + EPS) * w_ref[...]).astype(BF16)

    u = norm(h_ref).astype(F32)
    mirror_src = jnp.concatenate([norm(ha_ref), norm(hb_ref)], axis=0)
    um = jnp.dot(_mirror_select(l_tok), mirror_src, preferred_element_type=F32)
    us = (u + um).astype(BF16)
    ud = (u - um).astype(BF16)
    gw = FNET_GROUP_WIDTH
    for g in range(us.shape[1] // gw):
        cols = slice(g * gw, (g + 1) * gw)
        o_ref[0, :, cols] = jnp.dot(us[:, cols], c_ref[...], preferred_element_type=F32).astype(BF16)
        o_ref[1, :, cols] = jnp.dot(ud[:, cols], s_ref[...], preferred_element_type=F32).astype(BF16)


def _fnet_chan(h, w, c_tab, s_tab, l_tok, hp):
    b, lp, d = h.shape
    tm = FNET_TM
    nh = hp // tm
    assert lp % tm == 0 and hp % tm == 0 and d % FNET_GROUP_WIDTH == 0, (lp, hp, d)
    assert l_tok % 2 == 0 and lp > l_tok and l_tok >= nh * tm and hp > l_tok // 2, (l_tok, lp, hp)
    blk = (None, tm, d)
    return pl.pallas_call(
        functools.partial(_fnet_chan_kernel, l_tok=l_tok),
        grid=(b, nh),
        in_specs=[pl.BlockSpec(blk, lambda bi, i: (bi, i, 0)),
                  pl.BlockSpec(blk, lambda bi, i: (bi, _mirror_tile(l_tok, i), 0)),
                  pl.BlockSpec(blk, lambda bi, i: (bi, _mirror_tile(l_tok, i) + 1, 0)),
                  pl.BlockSpec((1, d), lambda bi, i: (0, 0)),
                  pl.BlockSpec(c_tab.shape, lambda bi, i: (0, 0)),
                  pl.BlockSpec(s_tab.shape, lambda bi, i: (0, 0))],
        out_specs=pl.BlockSpec((None, 2, tm, d), lambda bi, i: (bi, 0, i, 0)),
        out_shape=jax.ShapeDtypeStruct((b, 2, hp, d), BF16),
        compiler_params=_params(("parallel", "parallel"), 48),
        name="fnet_chan",
    )(h, h, h, w.reshape(1, d).astype(F32), c_tab, s_tab)


def _seq_dft_kernel(wc_ref, ws_ref, us_ref, ud_ref, o_ref):
    p = jnp.dot(wc_ref[...], us_ref[...], preferred_element_type=F32)
    q = jnp.dot(ws_ref[...], ud_ref[...], preferred_element_type=F32)
    o_ref[0] = (p - q).astype(o_ref.dtype)
    o_ref[1] = (p + q).astype(o_ref.dtype)


def _seq_dft(wc, ws, ab, tn=512):
    b, _, hp, d = ab.shape
    tm = next(t for t in (768, 512, 256) if hp % t == 0)
    assert d % tn == 0 and wc.shape == (hp, hp) and ws.shape == (hp, hp), (d, tn, wc.shape)
    return pl.pallas_call(
        _seq_dft_kernel,
        grid=(b, d // tn, hp // tm),
        in_specs=[pl.BlockSpec((tm, hp), lambda bi, j, i: (i, 0)),
                  pl.BlockSpec((tm, hp), lambda bi, j, i: (i, 0)),
                  pl.BlockSpec((None, None, hp, tn), lambda bi, j, i: (bi, 0, 0, j)),
                  pl.BlockSpec((None, None, hp, tn), lambda bi, j, i: (bi, 1, 0, j))],
        out_specs=pl.BlockSpec((None, 2, tm, tn), lambda bi, j, i: (bi, 0, i, j)),
        out_shape=jax.ShapeDtypeStruct((b, 2, hp, d), BF16),
        compiler_params=_params(("parallel", "parallel", "parallel"), 48),
        name="seq_dft",
    )(wc, ws, ab, ab)


def _fnet_out_kernel(fd_ref, fa_ref, fb_ref, w_ref, h_ref, o_ref, wbf, *, l_tok, tiles_per_seq):
    t = pl.program_id(1)

    @pl.when(t == 0)
    def _():
        wbf[...] = w_ref[...].astype(BF16)

    tm = FNET_TM
    mirror_src = jnp.concatenate([fa_ref[...], fb_ref[...]], axis=0)
    fm = jnp.dot(_mirror_select(l_tok), mirror_src, preferred_element_type=F32)
    row = (t % tiles_per_seq) * tm + lax.broadcasted_iota(jnp.int32, (tm, 1), 0)
    f = jnp.where(row <= l_tok // 2, fd_ref[...].astype(F32), fm)
    f = jnp.where(row < l_tok, f, 0.0).astype(BF16)
    o_ref[...] = h_ref[...] + jnp.dot(f, wbf[...], preferred_element_type=F32)


def _fnet_out(fm, w_stack, layer, h2d, l_tok, lp, tn=1024):
    b, _, hp, d = fm.shape
    tm = FNET_TM
    nh = hp // tm
    tps = lp // tm
    n = w_stack.shape[2]
    assert n % tn == 0 and h2d.shape == (b * lp, n), (n, tn, h2d.shape)
    clip = lambda v: jnp.clip(v, 0, nh - 1)
    blk = (None, None, tm, d)
    return pl.pallas_call(
        functools.partial(_fnet_out_kernel, l_tok=l_tok, tiles_per_seq=tps),
        grid=(n // tn, b * tps),
        in_specs=[pl.BlockSpec(blk, lambda j, t: (t // tps, 0, clip(t % tps), 0)),
                  pl.BlockSpec(blk, lambda j, t: (t // tps, 1, clip(_mirror_tile(l_tok, t % tps)), 0)),
                  pl.BlockSpec(blk, lambda j, t: (t // tps, 1, clip(_mirror_tile(l_tok, t % tps) + 1), 0)),
                  pl.BlockSpec((None, d, tn), lambda j, t: (layer, 0, j)),
                  pl.BlockSpec((tm, tn), lambda j, t: (t, j))],
        out_specs=pl.BlockSpec((tm, tn), lambda j, t: (t, j)),
        out_shape=jax.ShapeDtypeStruct(h2d.shape, F32),
        scratch_shapes=[pltpu.VMEM((d, tn), BF16)],
        input_output_aliases={4: 0},
        compiler_params=_params(("parallel", "arbitrary"), 48),
        name="fnet_out",
    )(fm, fm, fm, w_stack, h2d)


def _chan_dft_tables():
    n = FNET_GROUP_WIDTH
    j = jnp.arange(n, dtype=jnp.int32)
    th = ((j[:, None] * j[None, :]) % n).astype(F32) * (2.0 * math.pi / n)
    scale = 1.0 / math.sqrt(n)
    return (jnp.cos(th) * scale).astype(BF16), (jnp.sin(th) * scale).astype(BF16)


def _seq_dft_tables(l_tok, hp):
    blk = 64
    half = l_tok // 2
    k = jnp.arange(hp, dtype=jnp.int32)[:, None]
    a = jnp.arange(hp // blk, dtype=jnp.int32)[None, :] * blk
    b = jnp.arange(blk, dtype=jnp.int32)[None, :]
    w0 = 2.0 * math.pi / l_tok
    th1 = ((k * a) % l_tok).astype(F32) * w0
    th2 = ((k * b) % l_tok).astype(F32) * w0
    c1, s1 = jnp.cos(th1)[:, :, None], jnp.sin(th1)[:, :, None]
    c2, s2 = jnp.cos(th2)[:, None, :], jnp.sin(th2)[:, None, :]
    n = a[:, :, None] + b[:, None, :]
    valid = (k <= half)[:, :, None] & (n <= half)
    scale = 1.0 / math.sqrt(l_tok)
    col_w = jnp.where(n == half, 0.5 * scale, scale)
    wc = jnp.where(valid, (c1 * c2 - s1 * s2) * col_w, 0.0).reshape(hp, hp)
    ws = jnp.where(valid, (s1 * c2 + c1 * s2) * scale, 0.0).reshape(hp, hp)
    return wc.astype(BF16), ws.astype(BF16)


def _fourier_layer(h, l_tok, norm_w, w_out_stack, layer, chan_tabs, seq_tabs):
    b, lp, d = h.shape
    hp = seq_tabs[0].shape[0]
    ab = _fnet_chan(h, norm_w, chan_tabs[0], chan_tabs[1], l_tok, hp)
    fm = _seq_dft(seq_tabs[0], seq_tabs[1], ab)
    h2 = _fnet_out(fm, w_out_stack, layer, h.reshape(b * lp, d), l_tok, lp)
    return h2.reshape(b, lp, d)


def _split3(x):
    hi = x.astype(BF16)
    r = x - hi.astype(F32)
    mid = r.astype(BF16)
    lo = (r - mid.astype(F32)).astype(BF16)
    return hi, mid, lo


CONV_SUB = 512
SUBLANES = 8


def _conv_kernel(prev_ref, cur_ref, next_ref, w_ref, b_ref, o_ref, xs, *, l_tok):
    i = pl.program_id(1)
    n = pl.num_programs(1)
    rows = cur_ref.shape[0]
    lo = SUBLANES - CONV_WIDTH // 2
    row = i * rows + lax.broadcasted_iota(jnp.int32, (rows, CONV_SUB), 0)
    for s in range(cur_ref.shape[1] // CONV_SUB):
        cols = slice(s * CONV_SUB, (s + 1) * CONV_SUB)
        prev = prev_ref[:, cols].astype(F32)[HALO - SUBLANES:, :]
        nxt = next_ref[:, cols].astype(F32)[:SUBLANES, :]
        xs[0:SUBLANES, cols] = jnp.where(i > 0, prev, 0.0)
        xs[SUBLANES:SUBLANES + rows, cols] = cur_ref[:, cols].astype(F32)
        xs[SUBLANES + rows:2 * SUBLANES + rows, cols] = jnp.where(i < n - 1, nxt, 0.0)
        acc = b_ref[:, cols] + xs[lo:lo + rows, cols] * w_ref[0:1, cols]
        for k in range(1, CONV_WIDTH):
            acc = acc + xs[lo + k:lo + k + rows, cols] * w_ref[k:k + 1, cols]
        y = acc * jax.nn.sigmoid(acc)
        o_ref[:, cols] = jnp.where(row < l_tok, y, 0.0).astype(o_ref.dtype)


def _conv_silu(zxbc, conv_w, conv_b, l_tok, d_inner, tc=2048):
    b, lp, _ = zxbc.shape
    conv_dim = conv_w.shape[1]
    assert lp % CHUNK == 0 and conv_dim % tc == 0 and d_inner % tc == 0, (lp, conv_dim, d_inner)
    assert tc % CONV_SUB == 0 and HALO >= SUBLANES >= CONV_WIDTH // 2
    c0 = d_inner // tc
    hb = CHUNK // HALO
    nh = lp // HALO
    return pl.pallas_call(
        functools.partial(_conv_kernel, l_tok=l_tok),
        grid=(b, lp // CHUNK, conv_dim // tc),
        in_specs=[
            pl.BlockSpec((None, HALO, tc), lambda bi, i, j: (bi, jnp.maximum(i * hb - 1, 0), c0 + j)),
            pl.BlockSpec((None, CHUNK, tc), lambda bi, i, j: (bi, i, c0 + j)),
            pl.BlockSpec((None, HALO, tc), lambda bi, i, j: (bi, jnp.minimum((i + 1) * hb, nh - 1), c0 + j)),
            pl.BlockSpec((CONV_WIDTH, tc), lambda bi, i, j: (0, j)),
            pl.BlockSpec((1, tc), lambda bi, i, j: (0, j)),
        ],
        out_specs=pl.BlockSpec((None, CHUNK, tc), lambda bi, i, j: (bi, i, j)),
        out_shape=jax.ShapeDtypeStruct((b, lp, conv_dim), BF16),
        scratch_shapes=[pltpu.VMEM((CHUNK + 2 * SUBLANES, tc), F32)],
        compiler_params=_params(("parallel", "parallel", "parallel"), 32),
        name="conv_silu",
    )(zxbc, zxbc, zxbc, conv_w.astype(F32), conv_b.reshape(1, conv_dim).astype(F32))


def _dt_kernel(raw_ref, bias_ref, alog_ref, dt_ref, cs_ref, colg_ref, rowg_ref, *, l_tok):
    c = pl.program_id(1)
    x = raw_ref[...] + bias_ref[...]
    dt = jnp.maximum(x, 0.0) + jnp.log1p(jnp.exp(-jnp.abs(x)))
    row = c * CHUNK + lax.broadcasted_iota(jnp.int32, x.shape, 0)
    dt = jnp.where(row < l_tok, dt, 0.0)
    da = dt * (-jnp.exp(alog_ref[...]))
    li = lax.broadcasted_iota(jnp.int32, (CHUNK, CHUNK), 0)
    si = lax.broadcasted_iota(jnp.int32, (CHUNK, CHUNK), 1)
    tri_l = jnp.where(li >= si, 1.0, 0.0).astype(BF16)
    tri_u = jnp.where(li <= si, 1.0, 0.0).astype(BF16)
    hi, mid, lo = _split3(da)
    prefix = (jnp.dot(tri_l, hi, preferred_element_type=F32)
              + jnp.dot(tri_l, mid, preferred_element_type=F32)
              + jnp.dot(tri_l, lo, preferred_element_type=F32))
    suffix = (jnp.dot(tri_u, hi, preferred_element_type=F32)
              + jnp.dot(tri_u, mid, preferred_element_type=F32)
              + jnp.dot(tri_u, lo, preferred_element_type=F32))
    col = lax.broadcasted_iota(jnp.int32, x.shape, 1)
    cs = jnp.where((col & HEADS_PER_GROUP) != 0, suffix, prefix) * LOG2E
    dt_ref[...] = dt
    cs_ref[...] = cs
    src_t = (cs - jnp.log2(dt)).T
    w = 2 * HEADS_PER_GROUP
    for g in range(SSD_GROUPS):
        colg_ref[g] = cs[:, g * w:(g + 1) * w]
        rowg_ref[g] = src_t[g * w:(g + 1) * w, :]


def _dt_prepare(dt_raw, dt_bias, a_log, l_tok):
    b, lp, nh2 = dt_raw.shape
    assert lp % CHUNK == 0 and nh2 == 2 * HEADS_PER_GROUP * SSD_GROUPS, (lp, nh2)
    nc = lp // CHUNK
    w = 2 * HEADS_PER_GROUP
    blk = lambda bi, c: (bi, c, 0)
    return pl.pallas_call(
        functools.partial(_dt_kernel, l_tok=l_tok),
        grid=(b, nc),
        in_specs=[pl.BlockSpec((None, CHUNK, nh2), blk),
                  pl.BlockSpec((1, nh2), lambda bi, c: (0, 0)),
                  pl.BlockSpec((1, nh2), lambda bi, c: (0, 0))],
        out_specs=[pl.BlockSpec((None, CHUNK, nh2), blk),
                   pl.BlockSpec((None, CHUNK, nh2), blk),
                   pl.BlockSpec((None, None, SSD_GROUPS, CHUNK, w), lambda bi, c: (bi, c, 0, 0, 0)),
                   pl.BlockSpec((None, None, SSD_GROUPS, w, CHUNK), lambda bi, c: (bi, c, 0, 0, 0))],
        out_shape=[jax.ShapeDtypeStruct((b, lp, nh2), F32),
                   jax.ShapeDtypeStruct((b, lp, nh2), F32),
                   jax.ShapeDtypeStruct((b, nc, SSD_GROUPS, CHUNK, w), F32),
                   jax.ShapeDtypeStruct((b, nc, SSD_GROUPS, w, CHUNK), F32)],
        compiler_params=_params(("parallel", "parallel"), 32),
        name="dt_prepare",
    )(dt_raw, dt_bias.reshape(1, nh2).astype(F32), a_log.reshape(1, nh2).astype(F32))


def _state_kernel(xf_ref, bf_ref, dtf_ref, csf_ref, xb_ref, bb_ref, dtb_ref, csb_ref, e_ref,
                  hf_out, hb_out, hf, hb):
    t = pl.program_id(1)

    @pl.when(t == 0)
    def _():
        hf[...] = jnp.zeros_like(hf)
        hb[...] = jnp.zeros_like(hb)

    lane8 = lax.broadcasted_iota(jnp.int32, (SUBLANES, SSD_STATE), 1)

    def one_direction(x_ref, b_ref, dt_ref, cs_ref, h_out, h, total_row, d):
        cs = cs_ref[...]
        total = cs[total_row:total_row + 1, :]
        col = lax.broadcasted_iota(jnp.int32, cs.shape, 1)
        mine = (col & HEADS_PER_GROUP) == d * HEADS_PER_GROUP
        scale = (dt_ref[...] * jnp.exp2(jnp.where(mine, total - cs, 0.0))).astype(BF16)
        decay8 = jnp.broadcast_to(jnp.exp2(total), (SUBLANES, cs.shape[1]))
        for g in range(SSD_GROUPS):
            sc = jnp.dot(scale, e_ref[g, d], preferred_element_type=F32).astype(BF16)
            xdec = x_ref[:, g * GROUP_WIDTH:(g + 1) * GROUP_WIDTH] * sc
            s_new = lax.dot_general(b_ref[:, g * SSD_STATE:(g + 1) * SSD_STATE], xdec,
                                    (((0,), (0,)), ((), ())), preferred_element_type=F32)
            first = (2 * g + d) * HEADS_PER_GROUP
            dec = jnp.concatenate(
                [jnp.take_along_axis(decay8, first + 2 * j + (lane8 >> 6), axis=1)
                 for j in range(GROUP_WIDTH // SSD_STATE)], axis=1)[0:1, :]
            h_prev = h[g]
            h_out[g] = h_prev.astype(h_out.dtype)
            h[g] = h_prev * dec + s_new

    one_direction(xf_ref, bf_ref, dtf_ref, csf_ref, hf_out, hf, CHUNK - 1, 0)
    one_direction(xb_ref, bb_ref, dtb_ref, csb_ref, hb_out, hb, 0, 1)


def _head_expand_table():
    first = (jnp.arange(2 * SSD_GROUPS, dtype=jnp.int32) * HEADS_PER_GROUP)[:, None, None]
    r = jnp.arange(2 * SSD_GROUPS * HEADS_PER_GROUP, dtype=jnp.int32)[None, :, None]
    c = jnp.arange(GROUP_WIDTH, dtype=jnp.int32)[None, None, :]
    tab = jnp.where(r == first + c // SSD_HEAD_DIM, 1.0, 0.0).astype(BF16)
    return tab.reshape(SSD_GROUPS, 2, 2 * SSD_GROUPS * HEADS_PER_GROUP, GROUP_WIDTH)


def _ssd_states(xbc, dt, cs, expand, d_inner):
    b, lp, _ = xbc.shape
    nc = lp // CHUNK
    gn = SSD_GROUPS * SSD_STATE
    nh2 = dt.shape[2]
    fwd = lambda bi, t: (bi, t, 0)
    bwd = lambda bi, t: (bi, nc - 1 - t, 0)
    fwd_b = lambda bi, t: (bi, t, d_inner // gn)
    bwd_b = lambda bi, t: (bi, nc - 1 - t, d_inner // gn)
    st_shape = jax.ShapeDtypeStruct((b, nc, SSD_GROUPS, SSD_STATE, GROUP_WIDTH), BF16)
    st_block = (None, None, SSD_GROUPS, SSD_STATE, GROUP_WIDTH)
    return pl.pallas_call(
        _state_kernel,
        grid=(b, nc),
        in_specs=[pl.BlockSpec((None, CHUNK, d_inner), fwd),
                  pl.BlockSpec((None, CHUNK, gn), fwd_b),
                  pl.BlockSpec((None, CHUNK, nh2), fwd),
                  pl.BlockSpec((None, CHUNK, nh2), fwd),
                  pl.BlockSpec((None, CHUNK, d_inner), bwd),
                  pl.BlockSpec((None, CHUNK, gn), bwd_b),
                  pl.BlockSpec((None, CHUNK, nh2), bwd),
                  pl.BlockSpec((None, CHUNK, nh2), bwd),
                  pl.BlockSpec(expand.shape, lambda bi, t: (0, 0, 0, 0))],
        out_specs=[pl.BlockSpec(st_block, lambda bi, t: (bi, t, 0, 0, 0)),
                   pl.BlockSpec(st_block, lambda bi, t: (bi, nc - 1 - t, 0, 0, 0))],
        out_shape=[st_shape, st_shape],
        scratch_shapes=[pltpu.VMEM((SSD_GROUPS, SSD_STATE, GROUP_WIDTH), F32),
                        pltpu.VMEM((SSD_GROUPS, SSD_STATE, GROUP_WIDTH), F32)],
        compiler_params=_params(("parallel", "arbitrary"), 40),
        name="ssd_states",
    )(xbc, xbc, dt, cs, xbc, xbc, dt, cs, expand)


def _ssd_out_kernel(x_ref, b_ref, c_ref, z_ref, dt_ref, cs_ref, colg_ref, rowg_ref, hf_ref, hb_ref,
                    e_ref, dskip_ref, nw_ref, o_ref):
    hpg = HEADS_PER_GROUP
    half = CHUNK // 2
    lower = (lax.broadcasted_iota(jnp.int32, (half, half), 0)
             >= lax.broadcasted_iota(jnp.int32, (half, half), 1))
    lane = lax.broadcasted_iota(jnp.int32, (CHUNK, 2 * SSD_HEAD_DIM), 1)
    zero_bf = jnp.zeros((CHUNK, 2 * SSD_HEAD_DIM), BF16)
    edec = jnp.exp2(cs_ref[...]).astype(BF16)
    dt_bf = dt_ref[...].astype(BF16)

    def one_group(g, carry):
        wide = pl.ds(pl.multiple_of(g * GROUP_WIDTH, GROUP_WIDTH), GROUP_WIDTH)
        narrow = pl.ds(pl.multiple_of(g * SSD_STATE, SSD_STATE), SSD_STATE)
        x = x_ref[:, wide]
        cm = c_ref[:, narrow]
        bm = b_ref[:, narrow]
        cb = lax.dot_general(cm, bm, (((1,), (1,)), ((), ())),
                             preferred_element_type=F32).astype(BF16)
        colg = colg_ref[g]
        rowg = rowg_ref[g]
        ys = []
        for p in range(hpg // 2):
            gms = []
            for hh in (2 * p, 2 * p + 1):
                cf = colg[:, hh:hh + 1]
                sb = colg[:, hpg + hh:hpg + hh + 1]
                rf = rowg[hh:hh + 1, :]
                rb = rowg[hpg + hh:hpg + hh + 1, :]
                top = jnp.concatenate(
                    [jnp.where(lower, cf[:half] - rf[:, :half], sb[:half] - rb[:, :half]),
                     sb[:half] - rb[:, half:]], axis=1)
                bot = jnp.concatenate(
                    [cf[half:] - rf[:, :half],
                     jnp.where(lower, cf[half:] - rf[:, half:], sb[half:] - rb[:, half:])], axis=1)
                arg = jnp.concatenate([top, bot], axis=0)
                gms.append(jnp.exp2(arg).astype(BF16) * cb)
            g2 = jnp.concatenate(gms, axis=1)
            xp = x[:, p * 2 * SSD_HEAD_DIM:(p + 1) * 2 * SSD_HEAD_DIM]
            xbd = jnp.concatenate([jnp.where(lane < SSD_HEAD_DIM, xp, zero_bf),
                                   jnp.where(lane >= SSD_HEAD_DIM, xp, zero_bf)], axis=0)
            ys.append(jnp.dot(g2, xbd, preferred_element_type=F32))
        y = jnp.concatenate(ys, axis=1)

        sel_f = e_ref[g, 0]
        sel_b = e_ref[g, 1]
        diag_cb = jnp.sum(cm.astype(F32) * bm.astype(F32), axis=-1, keepdims=True)
        dt_b = jnp.dot(dt_bf, sel_b, preferred_element_type=F32)
        y = y + x.astype(F32) * (diag_cb * dt_b + dskip_ref[:, wide])
        ef = jnp.dot(edec, sel_f, preferred_element_type=F32)
        eb = jnp.dot(edec, sel_b, preferred_element_type=F32)
        y = y + jnp.dot(cm, hf_ref[g], preferred_element_type=F32) * ef
        y = y + jnp.dot(cm, hb_ref[g], preferred_element_type=F32) * eb

        z = z_ref[:, wide].astype(F32)
        gz = y * (z * jax.nn.sigmoid(z))
        ms = jnp.mean(gz * gz, axis=-1, keepdims=True)
        o_ref[:, wide] = (gz * lax.rsqrt(ms + EPS) * nw_ref[:, wide]).astype(o_ref.dtype)
        return carry

    lax.fori_loop(0, SSD_GROUPS, one_group, 0)


def _ssd_out(zxbc, xbc, dt, cs, colg, rowg, hf, hb, expand, d_skip, norm_w, d_inner):
    b, lp, _ = xbc.shape
    nc = lp // CHUNK
    w = 2 * HEADS_PER_GROUP
    gn = SSD_GROUPS * SSD_STATE
    assert d_inner == SSD_GROUPS * GROUP_WIDTH and d_inner % gn == 0 and xbc.shape[2] == d_inner + 2 * gn
    st_block = (None, None, SSD_GROUPS, SSD_STATE, GROUP_WIDTH)
    chunk_rows = lambda bi, c: (bi, c, 0)
    per_chunk = lambda bi, c: (bi, c, 0, 0, 0)
    return pl.pallas_call(
        _ssd_out_kernel,
        grid=(b, nc),
        in_specs=[
            pl.BlockSpec((None, CHUNK, d_inner), chunk_rows),
            pl.BlockSpec((None, CHUNK, gn), lambda bi, c: (bi, c, d_inner // gn)),
            pl.BlockSpec((None, CHUNK, gn), lambda bi, c: (bi, c, d_inner // gn + 1)),
            pl.BlockSpec((None, CHUNK, d_inner), chunk_rows),
            pl.BlockSpec((None, CHUNK, dt.shape[2]), chunk_rows),
            pl.BlockSpec((None, CHUNK, cs.shape[2]), chunk_rows),
            pl.BlockSpec((None, None, SSD_GROUPS, CHUNK, w), per_chunk),
            pl.BlockSpec((None, None, SSD_GROUPS, w, CHUNK), per_chunk),
            pl.BlockSpec(st_block, per_chunk),
            pl.BlockSpec(st_block, per_chunk),
            pl.BlockSpec(expand.shape, lambda bi, c: (0, 0, 0, 0)),
            pl.BlockSpec((1, d_inner), lambda bi, c: (0, 0)),
            pl.BlockSpec((1, d_inner), lambda bi, c: (0, 0)),
        ],
        out_specs=pl.BlockSpec((None, CHUNK, d_inner), chunk_rows),
        out_shape=jax.ShapeDtypeStruct((b, lp, d_inner), BF16),
        compiler_params=_params(("parallel", "parallel"), 40),
        name="ssd_out",
    )(xbc, xbc, xbc, zxbc, dt, cs, colg, rowg, hf, hb, expand,
      d_skip.reshape(1, d_inner).astype(F32), norm_w.reshape(1, d_inner).astype(F32))


def _group_major_dt_perm(n_heads):
    perm = []
    for g in range(SSD_GROUPS):
        for direction in range(2):
            for e in range(HEADS_PER_GROUP):
                perm.append(direction * n_heads + g * HEADS_PER_GROUP + e)
    return jnp.array(perm, dtype=jnp.int32)


def _ssd_layer(h, l_tok, layer, norm_w, w_in_stack, conv_w, conv_b, dt_bias, a_log, d_skip,
               gnorm_w, w_out_stack):
    b, lp, d = h.shape
    d_inner = w_out_stack.shape[1]
    n_heads = d_inner // SSD_HEAD_DIM
    n_main = d_inner + conv_w.shape[1]
    perm = _group_major_dt_perm(n_heads)
    h2d = h.reshape(b * lp, d)
    u = _rmsnorm(h2d, norm_w, BF16)
    zxbc = _matmul(u, w_in_stack, layer, n_main, BF16, 1024, 1024).reshape(b, lp, n_main)
    w_dt = jnp.take(w_in_stack[layer, :, n_main:], perm, axis=1)[None]
    dt_raw = _matmul(u, w_dt, 0, 2 * n_heads, F32, 1024, 2 * n_heads).reshape(b, lp, 2 * n_heads)
    xbc = _conv_silu(zxbc, conv_w, conv_b, l_tok, d_inner)
    dt, cs, colg, rowg = _dt_prepare(dt_raw, jnp.take(dt_bias.reshape(-1), perm),
                                     jnp.take(a_log.reshape(-1), perm), l_tok)
    expand = _head_expand_table()
    hf, hb = _ssd_states(xbc, dt, cs, expand, d_inner)
    yn = _ssd_out(zxbc, xbc, dt, cs, colg, rowg, hf, hb, expand,
                  jnp.repeat(d_skip, SSD_HEAD_DIM), gnorm_w, d_inner)
    h2 = _matmul(yn.reshape(b * lp, d_inner), w_out_stack, layer, d, F32, 512, 1024, res=h2d,
                 w_buffers=1, vmem_mb=52)
    return h2.reshape(b, lp, d)


def _ffn_layer(h, layer, norm_w, w_gate_stack, w_up_stack, w_down_stack):
    b, lp, d = h.shape
    h2d = h.reshape(b * lp, d)
    u = _rmsnorm(h2d, norm_w, BF16)
    act = _swiglu(u, w_gate_stack, w_up_stack, layer)
    h2 = _matmul(act, w_down_stack, layer, d, F32, 256, 1024, res=h2d, w_buffers=1, vmem_mb=56)
    return h2.reshape(b, lp, d)


def _trunk(x, meta_tokens, norm_mix_w, norm_ffn_w, norm_final_w, fnet_w_out,
           ssd_w_in, ssd_conv_w, ssd_conv_b, ssd_dt_bias, ssd_a_log, ssd_d,
           ssd_norm_w, ssd_w_out, ffn_w_gate, ffn_w_up, ffn_w_down):
    b, seq, d = x.shape
    n_meta = meta_tokens.shape[0]
    l_tok = n_meta + seq
    lp = -(-l_tok // ROW_PAD) * ROW_PAD
    depth = norm_mix_w.shape[0]
    meta = jnp.broadcast_to(meta_tokens.astype(x.dtype)[None], (b, n_meta, d))
    h = jnp.concatenate([meta, x, jnp.zeros((b, lp - l_tok, d), x.dtype)], axis=1)
    hp = -(-(l_tok // 2 + 1) // FNET_TM) * FNET_TM
    chan_tabs = _chan_dft_tables()
    seq_tabs = _seq_dft_tables(l_tok, hp)
    for i in range(depth):
        j = i // 2
        if i % 2 == 0:
            h = _fourier_layer(h, l_tok, norm_mix_w[i], fnet_w_out, j, chan_tabs, seq_tabs)
        else:
            h = _ssd_layer(h, l_tok, j, norm_mix_w[i], ssd_w_in, ssd_conv_w[j], ssd_conv_b[j],
                           ssd_dt_bias[j], ssd_a_log[j], ssd_d[j], ssd_norm_w[j], ssd_w_out)
        h = _ffn_layer(h, i, norm_ffn_w[i], ffn_w_gate, ffn_w_up, ffn_w_down)
    return _final_norm(h, norm_final_w, n_meta, seq)


def kernel(x, meta_tokens, norm_mix_w, norm_ffn_w, norm_final_w, fnet_w_out, ssd_w_in, ssd_conv_w, ssd_conv_b, ssd_dt_bias, ssd_a_log, ssd_d, ssd_norm_w, ssd_w_out, ffn_w_gate, ffn_w_up, ffn_w_down):
    return _trunk(x, meta_tokens, norm_mix_w, norm_ffn_w, norm_final_w, fnet_w_out,
                  ssd_w_in, ssd_conv_w, ssd_conv_b, ssd_dt_bias, ssd_a_log, ssd_d,
                  ssd_norm_w, ssd_w_out, ffn_w_gate, ffn_w_up, ffn_w_down)
```

```python
import functools
import math

import jax
import jax.numpy as jnp
from jax import lax
from jax.experimental import pallas as pl
from jax.experimental.pallas import tpu as pltpu

F32 = jnp.float32
BF16 = jnp.bfloat16

N_META = 16
FNET_GROUP_WIDTH = 256
SSD_HEAD_DIM = 64
SSD_GROUPS = 8
SSD_STATE = 128
HEADS_PER_GROUP = 8
GROUP_WIDTH = HEADS_PER_GROUP * SSD_HEAD_DIM
CONV_WIDTH = 5
CHUNK = 256
LOG2E = 1.4426950408889634
HALO = 16
EPS = 1e-6
ROW_PAD = 256


def _params(semantics, vmem_mb):
    return pltpu.CompilerParams(dimension_semantics=semantics,
                                vmem_limit_bytes=vmem_mb * 1024 * 1024)


def _rmsnorm_kernel(h_ref, w_ref, o_ref):
    x = h_ref[...]
    ms = jnp.mean(x * x, axis=-1, keepdims=True)
    o_ref[...] = (x * lax.rsqrt(ms + EPS) * w_ref[...]).astype(o_ref.dtype)


def _rmsnorm(h2d, w, out_dtype, tm=512):
    m, d = h2d.shape
    assert m % tm == 0, (m, tm)
    return pl.pallas_call(
        _rmsnorm_kernel,
        grid=(m // tm,),
        in_specs=[pl.BlockSpec((tm, d), lambda i: (i, 0)),
                  pl.BlockSpec((1, d), lambda i: (0, 0))],
        out_specs=pl.BlockSpec((tm, d), lambda i: (i, 0)),
        out_shape=jax.ShapeDtypeStruct((m, d), out_dtype),
        compiler_params=_params(("parallel",), 40),
        name="rmsnorm",
    )(h2d, w.reshape(1, d).astype(F32))


def _final_norm_kernel(h_ref, nxt_ref, w_ref, o_ref, *, shift):
    def norm(x):
        ms = jnp.mean(x * x, axis=-1, keepdims=True)
        return x * lax.rsqrt(ms + EPS) * w_ref[...]

    tm = o_ref.shape[0]
    o_ref[0:tm - shift, :] = norm(h_ref[shift:, :]).astype(o_ref.dtype)
    o_ref[tm - shift:, :] = norm(nxt_ref[...]).astype(o_ref.dtype)


def _final_norm(h, w, n_skip, seq, tm=256):
    b, lp, d = h.shape
    assert seq % tm == 0 and tm % n_skip == 0 and n_skip % SUBLANES == 0 and lp >= seq + tm, (seq, n_skip, lp)
    per = tm // n_skip
    return pl.pallas_call(
        functools.partial(_final_norm_kernel, shift=n_skip),
        grid=(b, seq // tm),
        in_specs=[pl.BlockSpec((None, tm, d), lambda bi, i: (bi, i, 0)),
                  pl.BlockSpec((None, n_skip, d), lambda bi, i: (bi, (i + 1) * per, 0)),
                  pl.BlockSpec((1, d), lambda bi, i: (0, 0))],
        out_specs=pl.BlockSpec((None, tm, d), lambda bi, i: (bi, i, 0)),
        out_shape=jax.ShapeDtypeStruct((b, seq, d), h.dtype),
        compiler_params=_params(("parallel", "parallel"), 40),
        name="final_norm",
    )(h, h, w.reshape(1, d).astype(F32))


def _mm_kernel(a_ref, w_ref, *rest, has_res):
    if has_res:
        r_ref, o_ref, wbf = rest
    else:
        o_ref, wbf = rest

    @pl.when(pl.program_id(1) == 0)
    def _():
        wbf[...] = w_ref[...].astype(BF16)

    acc = jnp.dot(a_ref[...], wbf[...], preferred_element_type=F32)
    if has_res:
        acc = r_ref[...] + acc
    o_ref[...] = acc.astype(o_ref.dtype)


def _matmul(a, w_stack, layer, n, out_dtype, tm, tn, col0=0, res=None, w_buffers=2, vmem_mb=48):
    m, k = a.shape
    assert w_stack.shape[1] == k, (w_stack.shape, k)
    assert m % tm == 0 and n % tn == 0 and col0 % tn == 0, (m, n, tm, tn, col0)
    cb = col0 // tn
    w_kwargs = {} if w_buffers == 2 else {"pipeline_mode": pl.Buffered(w_buffers)}
    in_specs = [pl.BlockSpec((tm, k), lambda j, i: (i, 0)),
                pl.BlockSpec((None, k, tn), lambda j, i: (layer, 0, cb + j), **w_kwargs)]
    args = [a, w_stack]
    aliases = {}
    if res is not None:
        in_specs.append(pl.BlockSpec((tm, tn), lambda j, i: (i, j)))
        args.append(res)
        aliases = {2: 0}
    return pl.pallas_call(
        functools.partial(_mm_kernel, has_res=res is not None),
        grid=(n // tn, m // tm),
        in_specs=in_specs,
        out_specs=pl.BlockSpec((tm, tn), lambda j, i: (i, j)),
        out_shape=jax.ShapeDtypeStruct((m, n), out_dtype),
        scratch_shapes=[pltpu.VMEM((k, tn), BF16)],
        input_output_aliases=aliases,
        compiler_params=_params(("parallel", "arbitrary"), vmem_mb),
        name="matmul_res" if res is not None else "matmul",
    )(*args)


def _swiglu_kernel(a_ref, wg_ref, wu_ref, o_ref, wg_bf, wu_bf):
    @pl.when(pl.program_id(1) == 0)
    def _():
        wg_bf[...] = wg_ref[...].astype(BF16)
        wu_bf[...] = wu_ref[...].astype(BF16)

    a = a_ref[...]
    gate = jnp.dot(a, wg_bf[...], preferred_element_type=F32)
    up = jnp.dot(a, wu_bf[...], preferred_element_type=F32)
    o_ref[...] = (gate * jax.nn.sigmoid(gate) * up).astype(o_ref.dtype)


def _swiglu(a, wg_stack, wu_stack, layer, tm=1024, tn=512):
    m, k = a.shape
    n = wg_stack.shape[2]
    assert m % tm == 0 and n % tn == 0, (m, n, tm, tn)
    w_spec = pl.BlockSpec((None, k, tn), lambda j, i: (layer, 0, j))
    return pl.pallas_call(
        _swiglu_kernel,
        grid=(n // tn, m // tm),
        in_specs=[pl.BlockSpec((tm, k), lambda j, i: (i, 0)), w_spec, w_spec],
        out_specs=pl.BlockSpec((tm, tn), lambda j, i: (i, j)),
        out_shape=jax.ShapeDtypeStruct((m, n), BF16),
        scratch_shapes=[pltpu.VMEM((k, tn), BF16), pltpu.VMEM((k, tn), BF16)],
        compiler_params=_params(("parallel", "arbitrary"), 48),
        name="swiglu",
    )(a, wg_stack, wu_stack)


FNET_TM = 256


def _mirror_tile(l_tok, i):
    return (l_tok - (i + 1) * FNET_TM) // FNET_TM


def _mirror_select(l_tok):
    r = lax.broadcasted_iota(jnp.int32, (FNET_TM, 2 * FNET_TM), 0)
    c = lax.broadcasted_iota(jnp.int32, (FNET_TM, 2 * FNET_TM), 1)
    return jnp.where(c == (l_tok % FNET_TM) + FNET_TM - r, 1.0, 0.0).astype(BF16)


def _fnet_chan_kernel(h_ref, ha_ref, hb_ref, w_ref, c_ref, s_ref, o_ref, *, l_tok):
    def norm(ref):
        x = ref[...]
        ms = jnp.mean(x * x, axis=-1, keepdims=True)
        return (x * lax.rsqrt(ms + EPS) * w_ref[...]).astype(BF16)

    u = norm(h_ref).astype(F32)
    mirror_src = jnp.concatenate([norm(ha_ref), norm(hb_ref)], axis=0)
    um = jnp.dot(_mirror_select(l_tok), mirror_src, preferred_element_type=F32)
    us = (u + um).astype(BF16)
    ud = (u - um).astype(BF16)
    gw = FNET_GROUP_WIDTH
    for g in range(us.shape[1] // gw):
        cols = slice(g * gw, (g + 1) * gw)
        o_ref[0, :, cols] = jnp.dot(us[:, cols], c_ref[...], preferred_element_type=F32).astype(BF16)
        o_ref[1, :, cols] = jnp.dot(ud[:, cols], s_ref[...], preferred_element_type=F32).astype(BF16)


def _fnet_chan(h, w, c_tab, s_tab, l_tok, hp):
    b, lp, d = h.shape
    tm = FNET_TM
    nh = hp // tm
    assert lp % tm == 0 and hp % tm == 0 and d % FNET_GROUP_WIDTH == 0, (lp, hp, d)
    assert l_tok % 2 == 0 and lp > l_tok and l_tok >= nh * tm and hp > l_tok // 2, (l_tok, lp, hp)
    blk = (None, tm, d)
    return pl.pallas_call(
        functools.partial(_fnet_chan_kernel, l_tok=l_tok),
        grid=(b, nh),
        in_specs=[pl.BlockSpec(blk, lambda bi, i: (bi, i, 0)),
                  pl.BlockSpec(blk, lambda bi, i: (bi, _mirror_tile(l_tok, i), 0)),
                  pl.BlockSpec(blk, lambda bi, i: (bi, _mirror_tile(l_tok, i) + 1, 0)),
                  pl.BlockSpec((1, d), lambda bi, i: (0, 0)),
                  pl.BlockSpec(c_tab.shape, lambda bi, i: (0, 0)),
                  pl.BlockSpec(s_tab.shape, lambda bi, i: (0, 0))],
        out_specs=pl.BlockSpec((None, 2, tm, d), lambda bi, i: (bi, 0, i, 0)),
        out_shape=jax.ShapeDtypeStruct((b, 2, hp, d), BF16),
        compiler_params=_params(("parallel", "parallel"), 48),
        name="fnet_chan",
    )(h, h, h, w.reshape(1, d).astype(F32), c_tab, s_tab)


def _seq_dft_kernel(wc_ref, ws_ref, us_ref, ud_ref, o_ref):
    p = jnp.dot(wc_ref[...], us_ref[...], preferred_element_type=F32)
    q = jnp.dot(ws_ref[...], ud_ref[...], preferred_element_type=F32)
    o_ref[0] = (p - q).astype(o_ref.dtype)
    o_ref[1] = (p + q).astype(o_ref.dtype)


def _seq_dft(wc, ws, ab, tn=512):
    b, _, hp, d = ab.shape
    tm = next(t for t in (768, 512, 256) if hp % t == 0)
    assert d % tn == 0 and wc.shape == (hp, hp) and ws.shape == (hp, hp), (d, tn, wc.shape)
    return pl.pallas_call(
        _seq_dft_kernel,
        grid=(b, d // tn, hp // tm),
        in_specs=[pl.BlockSpec((tm, hp), lambda bi, j, i: (i, 0)),
                  pl.BlockSpec((tm, hp), lambda bi, j, i: (i, 0)),
                  pl.BlockSpec((None, None, hp, tn), lambda bi, j, i: (bi, 0, 0, j)),
                  pl.BlockSpec((None, None, hp, tn), lambda bi, j, i: (bi, 1, 0, j))],
        out_specs=pl.BlockSpec((None, 2, tm, tn), lambda bi, j, i: (bi, 0, i, j)),
        out_shape=jax.ShapeDtypeStruct((b, 2, hp, d), BF16),
        compiler_params=_params(("parallel", "parallel", "parallel"), 48),
        name="seq_dft",
    )(wc, ws, ab, ab)


def _fnet_out_kernel(fd_ref, fa_ref, fb_ref, w_ref, h_ref, nw_ref, o_ref, u_ref, wbf, *,
                     l_tok, tiles_per_seq):
    t = pl.program_id(1)

    @pl.when(t == 0)
    def _():
        wbf[...] = w_ref[...].astype(BF16)

    tm = FNET_TM
    mirror_src = jnp.concatenate([fa_ref[...], fb_ref[...]], axis=0)
    fm = jnp.dot(_mirror_select(l_tok), mirror_src, preferred_element_type=F32)
    row = (t % tiles_per_seq) * tm + lax.broadcasted_iota(jnp.int32, (tm, 1), 0)
    f = jnp.where(row <= l_tok // 2, fd_ref[...].astype(F32), fm)
    f = jnp.where(row < l_tok, f, 0.0).astype(BF16)
    hn = h_ref[...] + jnp.dot(f, wbf[...], preferred_element_type=F32)
    o_ref[...] = hn
    ms = jnp.mean(hn * hn, axis=-1, keepdims=True)
    u_ref[...] = (hn * lax.rsqrt(ms + EPS) * nw_ref[...]).astype(u_ref.dtype)


def _fnet_out(fm, w_stack, layer, h2d, next_norm_w, l_tok, lp):
    b, _, hp, d = fm.shape
    tm = FNET_TM
    nh = hp // tm
    tps = lp // tm
    n = w_stack.shape[2]
    tn = n
    assert h2d.shape == (b * lp, n), (n, h2d.shape)
    clip = lambda v: jnp.clip(v, 0, nh - 1)
    blk = (None, None, tm, d)
    return pl.pallas_call(
        functools.partial(_fnet_out_kernel, l_tok=l_tok, tiles_per_seq=tps),
        grid=(n // tn, b * tps),
        in_specs=[pl.BlockSpec(blk, lambda j, t: (t // tps, 0, clip(t % tps), 0)),
                  pl.BlockSpec(blk, lambda j, t: (t // tps, 1, clip(_mirror_tile(l_tok, t % tps)), 0)),
                  pl.BlockSpec(blk, lambda j, t: (t // tps, 1, clip(_mirror_tile(l_tok, t % tps) + 1), 0)),
                  pl.BlockSpec((None, d, tn), lambda j, t: (layer, 0, j), pipeline_mode=pl.Buffered(1)),
                  pl.BlockSpec((tm, tn), lambda j, t: (t, j)),
                  pl.BlockSpec((1, tn), lambda j, t: (0, j))],
        out_specs=[pl.BlockSpec((tm, tn), lambda j, t: (t, j)),
                   pl.BlockSpec((tm, tn), lambda j, t: (t, j))],
        out_shape=[jax.ShapeDtypeStruct(h2d.shape, F32), jax.ShapeDtypeStruct(h2d.shape, BF16)],
        scratch_shapes=[pltpu.VMEM((d, tn), BF16)],
        input_output_aliases={4: 0},
        compiler_params=_params(("parallel", "arbitrary"), 56),
        name="fnet_out",
    )(fm, fm, fm, w_stack, h2d, next_norm_w.reshape(1, n).astype(F32))


def _chan_dft_tables():
    n = FNET_GROUP_WIDTH
    j = jnp.arange(n, dtype=jnp.int32)
    th = ((j[:, None] * j[None, :]) % n).astype(F32) * (2.0 * math.pi / n)
    scale = 1.0 / math.sqrt(n)
    return (jnp.cos(th) * scale).astype(BF16), (jnp.sin(th) * scale).astype(BF16)


def _seq_dft_tables(l_tok, hp):
    blk = 64
    half = l_tok // 2
    k = jnp.arange(hp, dtype=jnp.int32)[:, None]
    a = jnp.arange(hp // blk, dtype=jnp.int32)[None, :] * blk
    b = jnp.arange(blk, dtype=jnp.int32)[None, :]
    w0 = 2.0 * math.pi / l_tok
    th1 = ((k * a) % l_tok).astype(F32) * w0
    th2 = ((k * b) % l_tok).astype(F32) * w0
    c1, s1 = jnp.cos(th1)[:, :, None], jnp.sin(th1)[:, :, None]
    c2, s2 = jnp.cos(th2)[:, None, :], jnp.sin(th2)[:, None, :]
    n = a[:, :, None] + b[:, None, :]
    valid = (k <= half)[:, :, None] & (n <= half)
    scale = 1.0 / math.sqrt(l_tok)
    col_w = jnp.where(n == half, 0.5 * scale, scale)
    wc = jnp.where(valid, (c1 * c2 - s1 * s2) * col_w, 0.0).reshape(hp, hp)
    ws = jnp.where(valid, (s1 * c2 + c1 * s2) * scale, 0.0).reshape(hp, hp)
    return wc.astype(BF16), ws.astype(BF16)


def _fourier_layer(h, l_tok, norm_w, w_out_stack, layer, next_norm_w, chan_tabs, seq_tabs):
    b, lp, d = h.shape
    hp = seq_tabs[0].shape[0]
    ab = _fnet_chan(h, norm_w, chan_tabs[0], chan_tabs[1], l_tok, hp)
    fm = _seq_dft(seq_tabs[0], seq_tabs[1], ab)
    h2, u = _fnet_out(fm, w_out_stack, layer, h.reshape(b * lp, d), next_norm_w, l_tok, lp)
    return h2.reshape(b, lp, d), u


def _split3(x):
    hi = x.astype(BF16)
    r = x - hi.astype(F32)
    mid = r.astype(BF16)
    lo = (r - mid.astype(F32)).astype(BF16)
    return hi, mid, lo


CONV_SUB = 512
SUBLANES = 8


def _conv_kernel(prev_ref, cur_ref, next_ref, w_ref, b_ref, o_ref, *, l_tok):
    i = pl.program_id(1)
    n = pl.num_programs(1)
    rows = cur_ref.shape[0]
    ng = rows // SUBLANES
    half = CONV_WIDTH // 2
    sub = lax.broadcasted_iota(jnp.int32, (ng, SUBLANES, CONV_SUB), 1)
    row = i * rows + lax.broadcasted_iota(jnp.int32, (rows, CONV_SUB), 0)
    for s in range(cur_ref.shape[1] // CONV_SUB):
        cols = slice(s * CONV_SUB, (s + 1) * CONV_SUB)
        prev = jnp.where(i > 0, prev_ref[:, cols].astype(F32)[HALO - SUBLANES:, :], 0.0)
        nxt = jnp.where(i < n - 1, next_ref[:, cols].astype(F32)[:SUBLANES, :], 0.0)
        x = jnp.concatenate([prev, cur_ref[:, cols].astype(F32), nxt], axis=0)
        x3 = x.reshape(ng + 2, SUBLANES, CONV_SUB)
        acc = x3[1:ng + 1] * w_ref[half:half + 1, cols] + b_ref[:, cols]
        for d in range(1, half + 1):
            down = pltpu.roll(x3, d, 1)
            tap = jnp.where(sub >= d, down[1:ng + 1], down[0:ng])
            acc = acc + tap * w_ref[half - d:half - d + 1, cols]
            up = pltpu.roll(x3, SUBLANES - d, 1)
            tap = jnp.where(sub < SUBLANES - d, up[1:ng + 1], up[2:ng + 2])
            acc = acc + tap * w_ref[half + d:half + d + 1, cols]
        acc = acc.reshape(rows, CONV_SUB)
        y = acc * jax.nn.sigmoid(acc)
        o_ref[:, cols] = jnp.where(row < l_tok, y, 0.0).astype(o_ref.dtype)


def _conv_silu(zxbc, conv_w, conv_b, l_tok, d_inner, tc=2048):
    b, lp, _ = zxbc.shape
    conv_dim = conv_w.shape[1]
    assert lp % CHUNK == 0 and conv_dim % tc == 0 and d_inner % tc == 0, (lp, conv_dim, d_inner)
    assert tc % CONV_SUB == 0 and HALO >= SUBLANES >= CONV_WIDTH // 2
    c0 = d_inner // tc
    hb = CHUNK // HALO
    nh = lp // HALO
    return pl.pallas_call(
        functools.partial(_conv_kernel, l_tok=l_tok),
        grid=(b, lp // CHUNK, conv_dim // tc),
        in_specs=[
            pl.BlockSpec((None, HALO, tc), lambda bi, i, j: (bi, jnp.maximum(i * hb - 1, 0), c0 + j)),
            pl.BlockSpec((None, CHUNK, tc), lambda bi, i, j: (bi, i, c0 + j)),
            pl.BlockSpec((None, HALO, tc), lambda bi, i, j: (bi, jnp.minimum((i + 1) * hb, nh - 1), c0 + j)),
            pl.BlockSpec((CONV_WIDTH, tc), lambda bi, i, j: (0, j)),
            pl.BlockSpec((1, tc), lambda bi, i, j: (0, j)),
        ],
        out_specs=pl.BlockSpec((None, CHUNK, tc), lambda bi, i, j: (bi, i, j)),
        out_shape=jax.ShapeDtypeStruct((b, lp, conv_dim), BF16),
        compiler_params=_params(("parallel", "parallel", "parallel"), 32),
        name="conv_silu",
    )(zxbc, zxbc, zxbc, conv_w.astype(F32), conv_b.reshape(1, conv_dim).astype(F32))


def _dt_kernel(raw_ref, bias_ref, alog_ref, dt_ref, cs_ref, colg_ref, rowg_ref, *, l_tok):
    c = pl.program_id(1)
    x = raw_ref[...] + bias_ref[...]
    dt = jnp.maximum(x, 0.0) + jnp.log1p(jnp.exp(-jnp.abs(x)))
    row = c * CHUNK + lax.broadcasted_iota(jnp.int32, x.shape, 0)
    dt = jnp.where(row < l_tok, dt, 0.0)
    da = dt * (-jnp.exp(alog_ref[...]))
    li = lax.broadcasted_iota(jnp.int32, (CHUNK, CHUNK), 0)
    si = lax.broadcasted_iota(jnp.int32, (CHUNK, CHUNK), 1)
    tri_l = jnp.where(li >= si, 1.0, 0.0).astype(BF16)
    tri_u = jnp.where(li <= si, 1.0, 0.0).astype(BF16)
    hi, mid, lo = _split3(da)
    prefix = (jnp.dot(tri_l, hi, preferred_element_type=F32)
              + jnp.dot(tri_l, mid, preferred_element_type=F32)
              + jnp.dot(tri_l, lo, preferred_element_type=F32))
    suffix = (jnp.dot(tri_u, hi, preferred_element_type=F32)
              + jnp.dot(tri_u, mid, preferred_element_type=F32)
              + jnp.dot(tri_u, lo, preferred_element_type=F32))
    col = lax.broadcasted_iota(jnp.int32, x.shape, 1)
    cs = jnp.where((col & HEADS_PER_GROUP) != 0, suffix, prefix) * LOG2E
    dt_ref[...] = dt
    cs_ref[...] = cs
    src_t = (cs - jnp.log2(dt)).T
    w = 2 * HEADS_PER_GROUP
    for g in range(SSD_GROUPS):
        colg_ref[g] = cs[:, g * w:(g + 1) * w]
        rowg_ref[g] = src_t[g * w:(g + 1) * w, :]


def _dt_prepare(dt_raw, dt_bias, a_log, l_tok):
    b, lp, nh2 = dt_raw.shape
    assert lp % CHUNK == 0 and nh2 == 2 * HEADS_PER_GROUP * SSD_GROUPS, (lp, nh2)
    nc = lp // CHUNK
    w = 2 * HEADS_PER_GROUP
    blk = lambda bi, c: (bi, c, 0)
    return pl.pallas_call(
        functools.partial(_dt_kernel, l_tok=l_tok),
        grid=(b, nc),
        in_specs=[pl.BlockSpec((None, CHUNK, nh2), blk),
                  pl.BlockSpec((1, nh2), lambda bi, c: (0, 0)),
                  pl.BlockSpec((1, nh2), lambda bi, c: (0, 0))],
        out_specs=[pl.BlockSpec((None, CHUNK, nh2), blk),
                   pl.BlockSpec((None, CHUNK, nh2), blk),
                   pl.BlockSpec((None, None, SSD_GROUPS, CHUNK, w), lambda bi, c: (bi, c, 0, 0, 0)),
                   pl.BlockSpec((None, None, SSD_GROUPS, w, CHUNK), lambda bi, c: (bi, c, 0, 0, 0))],
        out_shape=[jax.ShapeDtypeStruct((b, lp, nh2), F32),
                   jax.ShapeDtypeStruct((b, lp, nh2), F32),
                   jax.ShapeDtypeStruct((b, nc, SSD_GROUPS, CHUNK, w), F32),
                   jax.ShapeDtypeStruct((b, nc, SSD_GROUPS, w, CHUNK), F32)],
        compiler_params=_params(("parallel", "parallel"), 32),
        name="dt_prepare",
    )(dt_raw, dt_bias.reshape(1, nh2).astype(F32), a_log.reshape(1, nh2).astype(F32))


def _state_kernel(xf_ref, bf_ref, dtf_ref, csf_ref, xb_ref, bb_ref, dtb_ref, csb_ref, e_ref,
                  hf_out, hb_out, hf, hb):
    t = pl.program_id(1)

    @pl.when(t == 0)
    def _():
        hf[...] = jnp.zeros_like(hf)
        hb[...] = jnp.zeros_like(hb)

    lane8 = lax.broadcasted_iota(jnp.int32, (SUBLANES, SSD_STATE), 1)

    def one_direction(x_ref, b_ref, dt_ref, cs_ref, h_out, h, total_row, d):
        cs = cs_ref[...]
        total = cs[total_row:total_row + 1, :]
        col = lax.broadcasted_iota(jnp.int32, cs.shape, 1)
        mine = (col & HEADS_PER_GROUP) == d * HEADS_PER_GROUP
        scale = (dt_ref[...] * jnp.exp2(jnp.where(mine, total - cs, 0.0))).astype(BF16)
        decay8 = jnp.broadcast_to(jnp.exp2(total), (SUBLANES, cs.shape[1]))
        for g in range(SSD_GROUPS):
            sc = jnp.dot(scale, e_ref[g, d], preferred_element_type=F32).astype(BF16)
            xdec = x_ref[:, g * GROUP_WIDTH:(g + 1) * GROUP_WIDTH] * sc
            s_new = lax.dot_general(b_ref[:, g * SSD_STATE:(g + 1) * SSD_STATE], xdec,
                                    (((0,), (0,)), ((), ())), preferred_element_type=F32)
            first = (2 * g + d) * HEADS_PER_GROUP
            dec = jnp.concatenate(
                [jnp.take_along_axis(decay8, first + 2 * j + (lane8 >> 6), axis=1)
                 for j in range(GROUP_WIDTH // SSD_STATE)], axis=1)[0:1, :]
            h_prev = h[g]
            h_out[g] = h_prev.astype(h_out.dtype)
            h[g] = h_prev * dec + s_new

    one_direction(xf_ref, bf_ref, dtf_ref, csf_ref, hf_out, hf, CHUNK - 1, 0)
    one_direction(xb_ref, bb_ref, dtb_ref, csb_ref, hb_out, hb, 0, 1)


def _head_expand_table():
    first = (jnp.arange(2 * SSD_GROUPS, dtype=jnp.int32) * HEADS_PER_GROUP)[:, None, None]
    r = jnp.arange(2 * SSD_GROUPS * HEADS_PER_GROUP, dtype=jnp.int32)[None, :, None]
    c = jnp.arange(GROUP_WIDTH, dtype=jnp.int32)[None, None, :]
    tab = jnp.where(r == first + c // SSD_HEAD_DIM, 1.0, 0.0).astype(BF16)
    return tab.reshape(SSD_GROUPS, 2, 2 * SSD_GROUPS * HEADS_PER_GROUP, GROUP_WIDTH)


def _ssd_states(xbc, dt, cs, expand, d_inner):
    b, lp, _ = xbc.shape
    nc = lp // CHUNK
    gn = SSD_GROUPS * SSD_STATE
    nh2 = dt.shape[2]
    fwd = lambda bi, t: (bi, t, 0)
    bwd = lambda bi, t: (bi, nc - 1 - t, 0)
    fwd_b = lambda bi, t: (bi, t, d_inner // gn)
    bwd_b = lambda bi, t: (bi, nc - 1 - t, d_inner // gn)
    st_shape = jax.ShapeDtypeStruct((b, nc, SSD_GROUPS, SSD_STATE, GROUP_WIDTH), BF16)
    st_block = (None, None, SSD_GROUPS, SSD_STATE, GROUP_WIDTH)
    return pl.pallas_call(
        _state_kernel,
        grid=(b, nc),
        in_specs=[pl.BlockSpec((None, CHUNK, d_inner), fwd),
                  pl.BlockSpec((None, CHUNK, gn), fwd_b),
                  pl.BlockSpec((None, CHUNK, nh2), fwd),
                  pl.BlockSpec((None, CHUNK, nh2), fwd),
                  pl.BlockSpec((None, CHUNK, d_inner), bwd),
                  pl.BlockSpec((None, CHUNK, gn), bwd_b),
                  pl.BlockSpec((None, CHUNK, nh2), bwd),
                  pl.BlockSpec((None, CHUNK, nh2), bwd),
                  pl.BlockSpec(expand.shape, lambda bi, t: (0, 0, 0, 0))],
        out_specs=[pl.BlockSpec(st_block, lambda bi, t: (bi, t, 0, 0, 0)),
                   pl.BlockSpec(st_block, lambda bi, t: (bi, nc - 1 - t, 0, 0, 0))],
        out_shape=[st_shape, st_shape],
        scratch_shapes=[pltpu.VMEM((SSD_GROUPS, SSD_STATE, GROUP_WIDTH), F32),
                        pltpu.VMEM((SSD_GROUPS, SSD_STATE, GROUP_WIDTH), F32)],
        compiler_params=_params(("parallel", "arbitrary"), 40),
        name="ssd_states",
    )(xbc, xbc, dt, cs, xbc, xbc, dt, cs, expand)


def _ssd_out_kernel(x_ref, b_ref, c_ref, z_ref, dt_ref, cs_ref, colg_ref, rowg_ref, hf_ref, hb_ref,
                    e_ref, dskip_ref, nw_ref, o_ref):
    hpg = HEADS_PER_GROUP
    half = CHUNK // 2
    lower = (lax.broadcasted_iota(jnp.int32, (half, half), 0)
             >= lax.broadcasted_iota(jnp.int32, (half, half), 1))
    lane = lax.broadcasted_iota(jnp.int32, (CHUNK, 2 * SSD_HEAD_DIM), 1)
    zero_bf = jnp.zeros((CHUNK, 2 * SSD_HEAD_DIM), BF16)
    edec = jnp.exp2(cs_ref[...]).astype(BF16)
    dt_bf = dt_ref[...].astype(BF16)

    def one_group(g, carry):
        wide = pl.ds(pl.multiple_of(g * GROUP_WIDTH, GROUP_WIDTH), GROUP_WIDTH)
        narrow = pl.ds(pl.multiple_of(g * SSD_STATE, SSD_STATE), SSD_STATE)
        x = x_ref[:, wide]
        cm = c_ref[:, narrow]
        bm = b_ref[:, narrow]
        cb = lax.dot_general(cm, bm, (((1,), (1,)), ((), ())),
                             preferred_element_type=F32).astype(BF16)
        colg = colg_ref[g]
        rowg = rowg_ref[g]
        ys = []
        for p in range(hpg // 2):
            gms = []
            for hh in (2 * p, 2 * p + 1):
                cf = colg[:, hh:hh + 1]
                sb = colg[:, hpg + hh:hpg + hh + 1]
                rf = rowg[hh:hh + 1, :]
                rb = rowg[hpg + hh:hpg + hh + 1, :]
                top = jnp.concatenate(
                    [jnp.where(lower, cf[:half] - rf[:, :half], sb[:half] - rb[:, :half]),
                     sb[:half] - rb[:, half:]], axis=1)
                bot = jnp.concatenate(
                    [cf[half:] - rf[:, :half],
                     jnp.where(lower, cf[half:] - rf[:, half:], sb[half:] - rb[:, half:])], axis=1)
                arg = jnp.concatenate([top, bot], axis=0)
                gms.append(jnp.exp2(arg).astype(BF16) * cb)
            g2 = jnp.concatenate(gms, axis=1)
            xp = x[:, p * 2 * SSD_HEAD_DIM:(p + 1) * 2 * SSD_HEAD_DIM]
            xbd = jnp.concatenate([jnp.where(lane < SSD_HEAD_DIM, xp, zero_bf),
                                   jnp.where(lane >= SSD_HEAD_DIM, xp, zero_bf)], axis=0)
            ys.append(jnp.dot(g2, xbd, preferred_element_type=F32))
        y = jnp.concatenate(ys, axis=1)

        sel_f = e_ref[g, 0]
        sel_b = e_ref[g, 1]
        diag_cb = jnp.sum(cm.astype(F32) * bm.astype(F32), axis=-1, keepdims=True)
        dt_b = jnp.dot(dt_bf, sel_b, preferred_element_type=F32)
        y = y + x.astype(F32) * (diag_cb * dt_b + dskip_ref[:, wide])
        ef = jnp.dot(edec, sel_f, preferred_element_type=F32)
        eb = jnp.dot(edec, sel_b, preferred_element_type=F32)
        y = y + jnp.dot(cm, hf_ref[g], preferred_element_type=F32) * ef
        y = y + jnp.dot(cm, hb_ref[g], preferred_element_type=F32) * eb

        z = z_ref[:, wide].astype(F32)
        gz = y * (z * jax.nn.sigmoid(z))
        ms = jnp.mean(gz * gz, axis=-1, keepdims=True)
        o_ref[:, wide] = (gz * lax.rsqrt(ms + EPS) * nw_ref[:, wide]).astype(o_ref.dtype)
        return carry

    lax.fori_loop(0, SSD_GROUPS, one_group, 0)


def _ssd_out(zxbc, xbc, dt, cs, colg, rowg, hf, hb, expand, d_skip, norm_w, d_inner):
    b, lp, _ = xbc.shape
    nc = lp // CHUNK
    w = 2 * HEADS_PER_GROUP
    gn = SSD_GROUPS * SSD_STATE
    assert d_inner == SSD_GROUPS * GROUP_WIDTH and d_inner % gn == 0 and xbc.shape[2] == d_inner + 2 * gn
    st_block = (None, None, SSD_GROUPS, SSD_STATE, GROUP_WIDTH)
    chunk_rows = lambda bi, c: (bi, c, 0)
    per_chunk = lambda bi, c: (bi, c, 0, 0, 0)
    return pl.pallas_call(
        _ssd_out_kernel,
        grid=(b, nc),
        in_specs=[
            pl.BlockSpec((None, CHUNK, d_inner), chunk_rows),
            pl.BlockSpec((None, CHUNK, gn), lambda bi, c: (bi, c, d_inner // gn)),
            pl.BlockSpec((None, CHUNK, gn), lambda bi, c: (bi, c, d_inner // gn + 1)),
            pl.BlockSpec((None, CHUNK, d_inner), chunk_rows),
            pl.BlockSpec((None, CHUNK, dt.shape[2]), chunk_rows),
            pl.BlockSpec((None, CHUNK, cs.shape[2]), chunk_rows),
            pl.BlockSpec((None, None, SSD_GROUPS, CHUNK, w), per_chunk),
            pl.BlockSpec((None, None, SSD_GROUPS, w, CHUNK), per_chunk),
            pl.BlockSpec(st_block, per_chunk),
            pl.BlockSpec(st_block, per_chunk),
            pl.BlockSpec(expand.shape, lambda bi, c: (0, 0, 0, 0)),
            pl.BlockSpec((1, d_inner), lambda bi, c: (0, 0)),
            pl.BlockSpec((1, d_inner), lambda bi, c: (0, 0)),
        ],
        out_specs=pl.BlockSpec((None, CHUNK, d_inner), chunk_rows),
        out_shape=jax.ShapeDtypeStruct((b, lp, d_inner), BF16),
        compiler_params=_params(("parallel", "parallel"), 40),
        name="ssd_out",
    )(xbc, xbc, xbc, zxbc, dt, cs, colg, rowg, hf, hb, expand,
      d_skip.reshape(1, d_inner).astype(F32), norm_w.reshape(1, d_inner).astype(F32))


def _group_major_dt_perm(n_heads):
    perm = []
    for g in range(SSD_GROUPS):
        for direction in range(2):
            for e in range(HEADS_PER_GROUP):
                perm.append(direction * n_heads + g * HEADS_PER_GROUP + e)
    return jnp.array(perm, dtype=jnp.int32)


def _ssd_layer(h, l_tok, layer, norm_w, w_in_stack, conv_w, conv_b, dt_bias, a_log, d_skip,
               gnorm_w, w_out_stack):
    b, lp, d = h.shape
    d_inner = w_out_stack.shape[1]
    n_heads = d_inner // SSD_HEAD_DIM
    n_main = d_inner + conv_w.shape[1]
    perm = _group_major_dt_perm(n_heads)
    h2d = h.reshape(b * lp, d)
    u = _rmsnorm(h2d, norm_w, BF16)
    zxbc = _matmul(u, w_in_stack, layer, n_main, BF16, 1024, 1024).reshape(b, lp, n_main)
    w_dt = jnp.take(w_in_stack[layer, :, n_main:], perm, axis=1)[None]
    dt_raw = _matmul(u, w_dt, 0, 2 * n_heads, F32, 1024, 2 * n_heads).reshape(b, lp, 2 * n_heads)
    xbc = _conv_silu(zxbc, conv_w, conv_b, l_tok, d_inner)
    dt, cs, colg, rowg = _dt_prepare(dt_raw, jnp.take(dt_bias.reshape(-1), perm),
                                     jnp.take(a_log.reshape(-1), perm), l_tok)
    expand = _head_expand_table()
    hf, hb = _ssd_states(xbc, dt, cs, expand, d_inner)
    yn = _ssd_out(zxbc, xbc, dt, cs, colg, rowg, hf, hb, expand,
                  jnp.repeat(d_skip, SSD_HEAD_DIM), gnorm_w, d_inner)
    h2 = _matmul(yn.reshape(b * lp, d_inner), w_out_stack, layer, d, F32, 512, 1024, res=h2d,
                 w_buffers=1, vmem_mb=52)
    return h2.reshape(b, lp, d)


def _ffn_layer(h, layer, norm_w, w_gate_stack, w_up_stack, w_down_stack, u=None):
    b, lp, d = h.shape
    h2d = h.reshape(b * lp, d)
    if u is None:
        u = _rmsnorm(h2d, norm_w, BF16)
    act = _swiglu(u, w_gate_stack, w_up_stack, layer)
    h2 = _matmul(act, w_down_stack, layer, d, F32, 256, 1024, res=h2d, w_buffers=1, vmem_mb=56)
    return h2.reshape(b, lp, d)


def _trunk(x, meta_tokens, norm_mix_w, norm_ffn_w, norm_final_w, fnet_w_out,
           ssd_w_in, ssd_conv_w, ssd_conv_b, ssd_dt_bias, ssd_a_log, ssd_d,
           ssd_norm_w, ssd_w_out, ffn_w_gate, ffn_w_up, ffn_w_down):
    b, seq, d = x.shape
    n_meta = meta_tokens.shape[0]
    l_tok = n_meta + seq
    lp = -(-l_tok // ROW_PAD) * ROW_PAD
    depth = norm_mix_w.shape[0]
    meta = jnp.broadcast_to(meta_tokens.astype(x.dtype)[None], (b, n_meta, d))
    h = jnp.concatenate([meta, x, jnp.zeros((b, lp - l_tok, d), x.dtype)], axis=1)
    hp = -(-(l_tok // 2 + 1) // FNET_TM) * FNET_TM
    chan_tabs = _chan_dft_tables()
    seq_tabs = _seq_dft_tables(l_tok, hp)
    for i in range(depth):
        j = i // 2
        u = None
        if i % 2 == 0:
            h, u = _fourier_layer(h, l_tok, norm_mix_w[i], fnet_w_out, j, norm_ffn_w[i],
                                  chan_tabs, seq_tabs)
        else:
            h = _ssd_layer(h, l_tok, j, norm_mix_w[i], ssd_w_in, ssd_conv_w[j], ssd_conv_b[j],
                           ssd_dt_bias[j], ssd_a_log[j], ssd_d[j], ssd_norm_w[j], ssd_w_out)
        h = _ffn_layer(h, i, norm_ffn_w[i], ffn_w_gate, ffn_w_up, ffn_w_down, u=u)
    return _final_norm(h, norm_final_w, n_meta, seq)


def kernel(x, meta_tokens, norm_mix_w, norm_ffn_w, norm_final_w, fnet_w_out, ssd_w_in, ssd_conv_w, ssd_conv_b, ssd_dt_bias, ssd_a_log, ssd_d, ssd_norm_w, ssd_w_out, ffn_w_gate, ffn_w_up, ffn_w_down):
    return _trunk(x, meta_tokens, norm_mix_w, norm_ffn_w, norm_final_w, fnet_w_out,
                  ssd_w_in, ssd_conv_w, ssd_conv_b, ssd_dt_bias, ssd_a_log, ssd_d,
                  ssd_norm_w, ssd_w_out, ffn_w_gate, ffn_w_up, ffn_w_down)
```

```python
import functools
import math

import jax
import jax.numpy as jnp
from jax import lax
from jax.experimental import pallas as pl
from jax.experimental.pallas import tpu as pltpu

F32 = jnp.float32
BF16 = jnp.bfloat16

N_META = 16
FNET_GROUP_WIDTH = 256
SSD_HEAD_DIM = 64
SSD_GROUPS = 8
SSD_STATE = 128
HEADS_PER_GROUP = 8
GROUP_WIDTH = HEADS_PER_GROUP * SSD_HEAD_DIM
CONV_WIDTH = 5
CHUNK = 256
LOG2E = 1.4426950408889634
HALO = 16
EPS = 1e-6
ROW_PAD = 256


def _params(semantics, vmem_mb):
    return pltpu.CompilerParams(dimension_semantics=semantics,
                                vmem_limit_bytes=vmem_mb * 1024 * 1024)


def _rmsnorm_kernel(h_ref, w_ref, o_ref):
    x = h_ref[...]
    ms = jnp.mean(x * x, axis=-1, keepdims=True)
    o_ref[...] = (x * lax.rsqrt(ms + EPS) * w_ref[...]).astype(o_ref.dtype)


def _rmsnorm(h2d, w, out_dtype, tm=512):
    m, d = h2d.shape
    assert m % tm == 0, (m, tm)
    return pl.pallas_call(
        _rmsnorm_kernel,
        grid=(m // tm,),
        in_specs=[pl.BlockSpec((tm, d), lambda i: (i, 0)),
                  pl.BlockSpec((1, d), lambda i: (0, 0))],
        out_specs=pl.BlockSpec((tm, d), lambda i: (i, 0)),
        out_shape=jax.ShapeDtypeStruct((m, d), out_dtype),
        compiler_params=_params(("parallel",), 40),
        name="rmsnorm",
    )(h2d, w.reshape(1, d).astype(F32))


def _final_norm_kernel(h_ref, nxt_ref, w_ref, o_ref, *, shift):
    def norm(x):
        ms = jnp.mean(x * x, axis=-1, keepdims=True)
        return x * lax.rsqrt(ms + EPS) * w_ref[...]

    tm = o_ref.shape[0]
    o_ref[0:tm - shift, :] = norm(h_ref[shift:, :]).astype(o_ref.dtype)
    o_ref[tm - shift:, :] = norm(nxt_ref[...]).astype(o_ref.dtype)


def _final_norm(h, w, n_skip, seq, tm=256):
    b, lp, d = h.shape
    assert seq % tm == 0 and tm % n_skip == 0 and n_skip % SUBLANES == 0 and lp >= seq + tm, (seq, n_skip, lp)
    per = tm // n_skip
    return pl.pallas_call(
        functools.partial(_final_norm_kernel, shift=n_skip),
        grid=(b, seq // tm),
        in_specs=[pl.BlockSpec((None, tm, d), lambda bi, i: (bi, i, 0)),
                  pl.BlockSpec((None, n_skip, d), lambda bi, i: (bi, (i + 1) * per, 0)),
                  pl.BlockSpec((1, d), lambda bi, i: (0, 0))],
        out_specs=pl.BlockSpec((None, tm, d), lambda bi, i: (bi, i, 0)),
        out_shape=jax.ShapeDtypeStruct((b, seq, d), h.dtype),
        compiler_params=_params(("parallel", "parallel"), 40),
        name="final_norm",
    )(h, h, w.reshape(1, d).astype(F32))


def _mm_kernel(a_ref, w_ref, *rest, has_res):
    if has_res:
        r_ref, o_ref, wbf = rest
    else:
        o_ref, wbf = rest

    @pl.when(pl.program_id(1) == 0)
    def _():
        wbf[...] = w_ref[...].astype(BF16)

    acc = jnp.dot(a_ref[...], wbf[...], preferred_element_type=F32)
    if has_res:
        acc = r_ref[...] + acc
    o_ref[...] = acc.astype(o_ref.dtype)


def _matmul(a, w_stack, layer, n, out_dtype, tm, tn, col0=0, res=None, w_buffers=2, vmem_mb=48):
    m, k = a.shape
    assert w_stack.shape[1] == k, (w_stack.shape, k)
    assert m % tm == 0 and n % tn == 0 and col0 % tn == 0, (m, n, tm, tn, col0)
    cb = col0 // tn
    w_kwargs = {} if w_buffers == 2 else {"pipeline_mode": pl.Buffered(w_buffers)}
    in_specs = [pl.BlockSpec((tm, k), lambda j, i: (i, 0)),
                pl.BlockSpec((None, k, tn), lambda j, i: (layer, 0, cb + j), **w_kwargs)]
    args = [a, w_stack]
    aliases = {}
    if res is not None:
        in_specs.append(pl.BlockSpec((tm, tn), lambda j, i: (i, j)))
        args.append(res)
        aliases = {2: 0}
    return pl.pallas_call(
        functools.partial(_mm_kernel, has_res=res is not None),
        grid=(n // tn, m // tm),
        in_specs=in_specs,
        out_specs=pl.BlockSpec((tm, tn), lambda j, i: (i, j)),
        out_shape=jax.ShapeDtypeStruct((m, n), out_dtype),
        scratch_shapes=[pltpu.VMEM((k, tn), BF16)],
        input_output_aliases=aliases,
        compiler_params=_params(("parallel", "arbitrary"), vmem_mb),
        name="matmul_res" if res is not None else "matmul",
    )(*args)


def _swiglu_kernel(a_ref, wg_ref, wu_ref, o_ref, wg_bf, wu_bf):
    @pl.when(pl.program_id(1) == 0)
    def _():
        wg_bf[...] = wg_ref[...].astype(BF16)
        wu_bf[...] = wu_ref[...].astype(BF16)

    a = a_ref[...]
    gate = jnp.dot(a, wg_bf[...], preferred_element_type=F32)
    up = jnp.dot(a, wu_bf[...], preferred_element_type=F32)
    o_ref[...] = (gate * jax.nn.sigmoid(gate) * up).astype(o_ref.dtype)


def _swiglu(a, wg_stack, wu_stack, layer, tm=1024, tn=512):
    m, k = a.shape
    n = wg_stack.shape[2]
    assert m % tm == 0 and n % tn == 0, (m, n, tm, tn)
    w_spec = pl.BlockSpec((None, k, tn), lambda j, i: (layer, 0, j))
    return pl.pallas_call(
        _swiglu_kernel,
        grid=(n // tn, m // tm),
        in_specs=[pl.BlockSpec((tm, k), lambda j, i: (i, 0)), w_spec, w_spec],
        out_specs=pl.BlockSpec((tm, tn), lambda j, i: (i, j)),
        out_shape=jax.ShapeDtypeStruct((m, n), BF16),
        scratch_shapes=[pltpu.VMEM((k, tn), BF16), pltpu.VMEM((k, tn), BF16)],
        compiler_params=_params(("parallel", "arbitrary"), 48),
        name="swiglu",
    )(a, wg_stack, wu_stack)


FNET_TM = 256


def _mirror_tile(l_tok, i):
    return (l_tok - (i + 1) * FNET_TM) // FNET_TM


def _mirror_select(l_tok):
    r = lax.broadcasted_iota(jnp.int32, (FNET_TM, 2 * FNET_TM), 0)
    c = lax.broadcasted_iota(jnp.int32, (FNET_TM, 2 * FNET_TM), 1)
    return jnp.where(c == (l_tok % FNET_TM) + FNET_TM - r, 1.0, 0.0).astype(BF16)


def _fnet_chan_kernel(h_ref, ha_ref, hb_ref, w_ref, c_ref, s_ref, o_ref, *, l_tok):
    def norm(ref):
        x = ref[...]
        ms = jnp.mean(x * x, axis=-1, keepdims=True)
        return (x * lax.rsqrt(ms + EPS) * w_ref[...]).astype(BF16)

    u = norm(h_ref).astype(F32)
    mirror_src = jnp.concatenate([norm(ha_ref), norm(hb_ref)], axis=0)
    um = jnp.dot(_mirror_select(l_tok), mirror_src, preferred_element_type=F32)
    us = (u + um).astype(BF16)
    ud = (u - um).astype(BF16)
    gw = FNET_GROUP_WIDTH
    for g in range(us.shape[1] // gw):
        cols = slice(g * gw, (g + 1) * gw)
        o_ref[0, :, cols] = jnp.dot(us[:, cols], c_ref[...], preferred_element_type=F32).astype(BF16)
        o_ref[1, :, cols] = jnp.dot(ud[:, cols], s_ref[...], preferred_element_type=F32).astype(BF16)


def _fnet_chan(h, w, c_tab, s_tab, l_tok, hp):
    b, lp, d = h.shape
    tm = FNET_TM
    nh = hp // tm
    assert lp % tm == 0 and hp % tm == 0 and d % FNET_GROUP_WIDTH == 0, (lp, hp, d)
    assert l_tok % 2 == 0 and lp > l_tok and l_tok >= nh * tm and hp > l_tok // 2, (l_tok, lp, hp)
    blk = (None, tm, d)
    return pl.pallas_call(
        functools.partial(_fnet_chan_kernel, l_tok=l_tok),
        grid=(b, nh),
        in_specs=[pl.BlockSpec(blk, lambda bi, i: (bi, i, 0)),
                  pl.BlockSpec(blk, lambda bi, i: (bi, _mirror_tile(l_tok, i), 0)),
                  pl.BlockSpec(blk, lambda bi, i: (bi, _mirror_tile(l_tok, i) + 1, 0)),
                  pl.BlockSpec((1, d), lambda bi, i: (0, 0)),
                  pl.BlockSpec(c_tab.shape, lambda bi, i: (0, 0)),
                  pl.BlockSpec(s_tab.shape, lambda bi, i: (0, 0))],
        out_specs=pl.BlockSpec((None, 2, tm, d), lambda bi, i: (bi, 0, i, 0)),
        out_shape=jax.ShapeDtypeStruct((b, 2, hp, d), BF16),
        compiler_params=_params(("parallel", "parallel"), 48),
        name="fnet_chan",
    )(h, h, h, w.reshape(1, d).astype(F32), c_tab, s_tab)


def _seq_dft_kernel(wc_ref, ws_ref, us_ref, ud_ref, o_ref):
    p = jnp.dot(wc_ref[...], us_ref[...], preferred_element_type=F32)
    q = jnp.dot(ws_ref[...], ud_ref[...], preferred_element_type=F32)
    o_ref[0] = (p - q).astype(o_ref.dtype)
    o_ref[1] = (p + q).astype(o_ref.dtype)


def _seq_dft(wc, ws, ab, tn=512):
    b, _, hp, d = ab.shape
    tm = next(t for t in (768, 512, 256) if hp % t == 0)
    assert d % tn == 0 and wc.shape == (hp, hp) and ws.shape == (hp, hp), (d, tn, wc.shape)
    return pl.pallas_call(
        _seq_dft_kernel,
        grid=(b, d // tn, hp // tm),
        in_specs=[pl.BlockSpec((tm, hp), lambda bi, j, i: (i, 0)),
                  pl.BlockSpec((tm, hp), lambda bi, j, i: (i, 0)),
                  pl.BlockSpec((None, None, hp, tn), lambda bi, j, i: (bi, 0, 0, j)),
                  pl.BlockSpec((None, None, hp, tn), lambda bi, j, i: (bi, 1, 0, j))],
        out_specs=pl.BlockSpec((None, 2, tm, tn), lambda bi, j, i: (bi, 0, i, j)),
        out_shape=jax.ShapeDtypeStruct((b, 2, hp, d), BF16),
        compiler_params=_params(("parallel", "parallel", "parallel"), 48),
        name="seq_dft",
    )(wc, ws, ab, ab)


def _fnet_out_kernel(fd_ref, fa_ref, fb_ref, w_ref, h_ref, nw_ref, o_ref, u_ref, wbf, *,
                     l_tok, tiles_per_seq):
    t = pl.program_id(1)

    @pl.when(t == 0)
    def _():
        wbf[...] = w_ref[...].astype(BF16)

    tm = FNET_TM
    mirror_src = jnp.concatenate([fa_ref[...], fb_ref[...]], axis=0)
    fm = jnp.dot(_mirror_select(l_tok), mirror_src, preferred_element_type=F32)
    row = (t % tiles_per_seq) * tm + lax.broadcasted_iota(jnp.int32, (tm, 1), 0)
    f = jnp.where(row <= l_tok // 2, fd_ref[...].astype(F32), fm)
    f = jnp.where(row < l_tok, f, 0.0).astype(BF16)
    hn = h_ref[...] + jnp.dot(f, wbf[...], preferred_element_type=F32)
    o_ref[...] = hn
    ms = jnp.mean(hn * hn, axis=-1, keepdims=True)
    u_ref[...] = (hn * lax.rsqrt(ms + EPS) * nw_ref[...]).astype(u_ref.dtype)


def _fnet_out(fm, w_stack, layer, h2d, next_norm_w, l_tok, lp):
    b, _, hp, d = fm.shape
    tm = FNET_TM
    nh = hp // tm
    tps = lp // tm
    n = w_stack.shape[2]
    tn = n
    assert h2d.shape == (b * lp, n), (n, h2d.shape)
    clip = lambda v: jnp.clip(v, 0, nh - 1)
    blk = (None, None, tm, d)
    return pl.pallas_call(
        functools.partial(_fnet_out_kernel, l_tok=l_tok, tiles_per_seq=tps),
        grid=(n // tn, b * tps),
        in_specs=[pl.BlockSpec(blk, lambda j, t: (t // tps, 0, clip(t % tps), 0)),
                  pl.BlockSpec(blk, lambda j, t: (t // tps, 1, clip(_mirror_tile(l_tok, t % tps)), 0)),
                  pl.BlockSpec(blk, lambda j, t: (t // tps, 1, clip(_mirror_tile(l_tok, t % tps) + 1), 0)),
                  pl.BlockSpec((None, d, tn), lambda j, t: (layer, 0, j), pipeline_mode=pl.Buffered(1)),
                  pl.BlockSpec((tm, tn), lambda j, t: (t, j)),
                  pl.BlockSpec((1, tn), lambda j, t: (0, j))],
        out_specs=[pl.BlockSpec((tm, tn), lambda j, t: (t, j)),
                   pl.BlockSpec((tm, tn), lambda j, t: (t, j))],
        out_shape=[jax.ShapeDtypeStruct(h2d.shape, F32), jax.ShapeDtypeStruct(h2d.shape, BF16)],
        scratch_shapes=[pltpu.VMEM((d, tn), BF16)],
        input_output_aliases={4: 0},
        compiler_params=_params(("parallel", "arbitrary"), 56),
        name="fnet_out",
    )(fm, fm, fm, w_stack, h2d, next_norm_w.reshape(1, n).astype(F32))


def _chan_dft_tables():
    n = FNET_GROUP_WIDTH
    j = jnp.arange(n, dtype=jnp.int32)
    th = ((j[:, None] * j[None, :]) % n).astype(F32) * (2.0 * math.pi / n)
    scale = 1.0 / math.sqrt(n)
    return (jnp.cos(th) * scale).astype(BF16), (jnp.sin(th) * scale).astype(BF16)


def _seq_dft_tables(l_tok, hp):
    blk = 64
    half = l_tok // 2
    k = jnp.arange(hp, dtype=jnp.int32)[:, None]
    a = jnp.arange(hp // blk, dtype=jnp.int32)[None, :] * blk
    b = jnp.arange(blk, dtype=jnp.int32)[None, :]
    w0 = 2.0 * math.pi / l_tok
    th1 = ((k * a) % l_tok).astype(F32) * w0
    th2 = ((k * b) % l_tok).astype(F32) * w0
    c1, s1 = jnp.cos(th1)[:, :, None], jnp.sin(th1)[:, :, None]
    c2, s2 = jnp.cos(th2)[:, None, :], jnp.sin(th2)[:, None, :]
    n = a[:, :, None] + b[:, None, :]
    valid = (k <= half)[:, :, None] & (n <= half)
    scale = 1.0 / math.sqrt(l_tok)
    col_w = jnp.where(n == half, 0.5 * scale, scale)
    wc = jnp.where(valid, (c1 * c2 - s1 * s2) * col_w, 0.0).reshape(hp, hp)
    ws = jnp.where(valid, (s1 * c2 + c1 * s2) * scale, 0.0).reshape(hp, hp)
    return wc.astype(BF16), ws.astype(BF16)


def _fourier_layer(h, l_tok, norm_w, w_out_stack, layer, next_norm_w, chan_tabs, seq_tabs):
    b, lp, d = h.shape
    hp = seq_tabs[0].shape[0]
    ab = _fnet_chan(h, norm_w, chan_tabs[0], chan_tabs[1], l_tok, hp)
    fm = _seq_dft(seq_tabs[0], seq_tabs[1], ab)
    h2, u = _fnet_out(fm, w_out_stack, layer, h.reshape(b * lp, d), next_norm_w, l_tok, lp)
    return h2.reshape(b, lp, d), u


def _split3(x):
    hi = x.astype(BF16)
    r = x - hi.astype(F32)
    mid = r.astype(BF16)
    lo = (r - mid.astype(F32)).astype(BF16)
    return hi, mid, lo


CONV_SUB = 512
SUBLANES = 8


def _conv_kernel(prev_ref, cur_ref, next_ref, w_ref, b_ref, o_ref, *, l_tok):
    i = pl.program_id(1)
    n = pl.num_programs(1)
    rows = cur_ref.shape[0]
    ng = rows // SUBLANES
    half = CONV_WIDTH // 2
    sub = lax.broadcasted_iota(jnp.int32, (ng, SUBLANES, CONV_SUB), 1)
    row = i * rows + lax.broadcasted_iota(jnp.int32, (rows, CONV_SUB), 0)
    for s in range(cur_ref.shape[1] // CONV_SUB):
        cols = slice(s * CONV_SUB, (s + 1) * CONV_SUB)
        prev = jnp.where(i > 0, prev_ref[:, cols].astype(F32)[HALO - SUBLANES:, :], 0.0)
        nxt = jnp.where(i < n - 1, next_ref[:, cols].astype(F32)[:SUBLANES, :], 0.0)
        x = jnp.concatenate([prev, cur_ref[:, cols].astype(F32), nxt], axis=0)
        x3 = x.reshape(ng + 2, SUBLANES, CONV_SUB)
        acc = x3[1:ng + 1] * w_ref[half:half + 1, cols] + b_ref[:, cols]
        for d in range(1, half + 1):
            down = pltpu.roll(x3, d, 1)
            tap = jnp.where(sub >= d, down[1:ng + 1], down[0:ng])
            acc = acc + tap * w_ref[half - d:half - d + 1, cols]
            up = pltpu.roll(x3, SUBLANES - d, 1)
            tap = jnp.where(sub < SUBLANES - d, up[1:ng + 1], up[2:ng + 2])
            acc = acc + tap * w_ref[half + d:half + d + 1, cols]
        acc = acc.reshape(rows, CONV_SUB)
        y = acc * jax.nn.sigmoid(acc)
        o_ref[:, cols] = jnp.where(row < l_tok, y, 0.0).astype(o_ref.dtype)


def _conv_silu(zxbc, conv_w, conv_b, l_tok, d_inner, tc=2048):
    b, lp, _ = zxbc.shape
    conv_dim = conv_w.shape[1]
    assert lp % CHUNK == 0 and conv_dim % tc == 0 and d_inner % tc == 0, (lp, conv_dim, d_inner)
    assert tc % CONV_SUB == 0 and HALO >= SUBLANES >= CONV_WIDTH // 2
    c0 = d_inner // tc
    hb = CHUNK // HALO
    nh = lp // HALO
    return pl.pallas_call(
        functools.partial(_conv_kernel, l_tok=l_tok),
        grid=(b, lp // CHUNK, conv_dim // tc),
        in_specs=[
            pl.BlockSpec((None, HALO, tc), lambda bi, i, j: (bi, jnp.maximum(i * hb - 1, 0), c0 + j)),
            pl.BlockSpec((None, CHUNK, tc), lambda bi, i, j: (bi, i, c0 + j)),
            pl.BlockSpec((None, HALO, tc), lambda bi, i, j: (bi, jnp.minimum((i + 1) * hb, nh - 1), c0 + j)),
            pl.BlockSpec((CONV_WIDTH, tc), lambda bi, i, j: (0, j)),
            pl.BlockSpec((1, tc), lambda bi, i, j: (0, j)),
        ],
        out_specs=pl.BlockSpec((None, CHUNK, tc), lambda bi, i, j: (bi, i, j)),
        out_shape=jax.ShapeDtypeStruct((b, lp, conv_dim), BF16),
        compiler_params=_params(("parallel", "parallel", "parallel"), 32),
        name="conv_silu",
    )(zxbc, zxbc, zxbc, conv_w.astype(F32), conv_b.reshape(1, conv_dim).astype(F32))


def _dt_kernel(raw_ref, bias_ref, alog_ref, dt_ref, cs_ref, colg_ref, rowg_ref, *, l_tok):
    c = pl.program_id(1)
    x = raw_ref[...] + bias_ref[...]
    dt = jnp.maximum(x, 0.0) + jnp.log1p(jnp.exp(-jnp.abs(x)))
    row = c * CHUNK + lax.broadcasted_iota(jnp.int32, x.shape, 0)
    dt = jnp.where(row < l_tok, dt, 0.0)
    da = dt * (-jnp.exp(alog_ref[...]))
    li = lax.broadcasted_iota(jnp.int32, (CHUNK, CHUNK), 0)
    si = lax.broadcasted_iota(jnp.int32, (CHUNK, CHUNK), 1)
    tri_l = jnp.where(li >= si, 1.0, 0.0).astype(BF16)
    tri_u = jnp.where(li <= si, 1.0, 0.0).astype(BF16)
    hi, mid, lo = _split3(da)
    prefix = (jnp.dot(tri_l, hi, preferred_element_type=F32)
              + jnp.dot(tri_l, mid, preferred_element_type=F32)
              + jnp.dot(tri_l, lo, preferred_element_type=F32))
    suffix = (jnp.dot(tri_u, hi, preferred_element_type=F32)
              + jnp.dot(tri_u, mid, preferred_element_type=F32)
              + jnp.dot(tri_u, lo, preferred_element_type=F32))
    col = lax.broadcasted_iota(jnp.int32, x.shape, 1)
    cs = jnp.where((col & HEADS_PER_GROUP) != 0, suffix, prefix) * LOG2E
    dt_ref[...] = dt
    cs_ref[...] = cs
    src_t = (cs - jnp.log2(dt)).T
    w = 2 * HEADS_PER_GROUP
    for g in range(SSD_GROUPS):
        colg_ref[g] = cs[:, g * w:(g + 1) * w]
        rowg_ref[g] = src_t[g * w:(g + 1) * w, :]


def _dt_prepare(dt_raw, dt_bias, a_log, l_tok):
    b, lp, nh2 = dt_raw.shape
    assert lp % CHUNK == 0 and nh2 == 2 * HEADS_PER_GROUP * SSD_GROUPS, (lp, nh2)
    nc = lp // CHUNK
    w = 2 * HEADS_PER_GROUP
    blk = lambda bi, c: (bi, c, 0)
    return pl.pallas_call(
        functools.partial(_dt_kernel, l_tok=l_tok),
        grid=(b, nc),
        in_specs=[pl.BlockSpec((None, CHUNK, nh2), blk),
                  pl.BlockSpec((1, nh2), lambda bi, c: (0, 0)),
                  pl.BlockSpec((1, nh2), lambda bi, c: (0, 0))],
        out_specs=[pl.BlockSpec((None, CHUNK, nh2), blk),
                   pl.BlockSpec((None, CHUNK, nh2), blk),
                   pl.BlockSpec((None, None, SSD_GROUPS, CHUNK, w), lambda bi, c: (bi, c, 0, 0, 0)),
                   pl.BlockSpec((None, None, SSD_GROUPS, w, CHUNK), lambda bi, c: (bi, c, 0, 0, 0))],
        out_shape=[jax.ShapeDtypeStruct((b, lp, nh2), F32),
                   jax.ShapeDtypeStruct((b, lp, nh2), F32),
                   jax.ShapeDtypeStruct((b, nc, SSD_GROUPS, CHUNK, w), F32),
                   jax.ShapeDtypeStruct((b, nc, SSD_GROUPS, w, CHUNK), F32)],
        compiler_params=_params(("parallel", "parallel"), 32),
        name="dt_prepare",
    )(dt_raw, dt_bias.reshape(1, nh2).astype(F32), a_log.reshape(1, nh2).astype(F32))


def _state_kernel(xf_ref, bf_ref, dtf_ref, csf_ref, xb_ref, bb_ref, dtb_ref, csb_ref, e_ref,
                  hf_out, hb_out, hf, hb):
    t = pl.program_id(1)

    @pl.when(t == 0)
    def _():
        hf[...] = jnp.zeros_like(hf)
        hb[...] = jnp.zeros_like(hb)

    lane8 = lax.broadcasted_iota(jnp.int32, (SUBLANES, SSD_STATE), 1)

    def one_direction(x_ref, b_ref, dt_ref, cs_ref, h_out, h, total_row, d):
        cs = cs_ref[...]
        total = cs[total_row:total_row + 1, :]
        col = lax.broadcasted_iota(jnp.int32, cs.shape, 1)
        mine = (col & HEADS_PER_GROUP) == d * HEADS_PER_GROUP
        scale = (dt_ref[...] * jnp.exp2(jnp.where(mine, total - cs, 0.0))).astype(BF16)
        decay8 = jnp.broadcast_to(jnp.exp2(total), (SUBLANES, cs.shape[1]))
        xdec = x_ref[...] * jnp.dot(scale, e_ref[d], preferred_element_type=F32).astype(BF16)
        for g in range(SSD_GROUPS):
            s_new = lax.dot_general(b_ref[:, g * SSD_STATE:(g + 1) * SSD_STATE],
                                    xdec[:, g * GROUP_WIDTH:(g + 1) * GROUP_WIDTH],
                                    (((0,), (0,)), ((), ())), preferred_element_type=F32)
            first = (2 * g + d) * HEADS_PER_GROUP
            dec = jnp.concatenate(
                [jnp.take_along_axis(decay8, first + 2 * j + (lane8 >> 6), axis=1)
                 for j in range(GROUP_WIDTH // SSD_STATE)], axis=1)[0:1, :]
            h_prev = h[g]
            h_out[g] = h_prev.astype(h_out.dtype)
            h[g] = h_prev * dec + s_new

    one_direction(xf_ref, bf_ref, dtf_ref, csf_ref, hf_out, hf, CHUNK - 1, 0)
    one_direction(xb_ref, bb_ref, dtb_ref, csb_ref, hb_out, hb, 0, 1)


def _head_expand_table():
    d = jnp.arange(2, dtype=jnp.int32)[:, None, None]
    r = jnp.arange(2 * SSD_GROUPS * HEADS_PER_GROUP, dtype=jnp.int32)[None, :, None]
    c = jnp.arange(SSD_GROUPS * GROUP_WIDTH, dtype=jnp.int32)[None, None, :]
    src = (2 * (c // GROUP_WIDTH) + d) * HEADS_PER_GROUP + (c % GROUP_WIDTH) // SSD_HEAD_DIM
    return jnp.where(r == src, 1.0, 0.0).astype(BF16)


def _ssd_states(xbc, dt, cs, expand, d_inner):
    b, lp, _ = xbc.shape
    nc = lp // CHUNK
    gn = SSD_GROUPS * SSD_STATE
    nh2 = dt.shape[2]
    fwd = lambda bi, t: (bi, t, 0)
    bwd = lambda bi, t: (bi, nc - 1 - t, 0)
    fwd_b = lambda bi, t: (bi, t, d_inner // gn)
    bwd_b = lambda bi, t: (bi, nc - 1 - t, d_inner // gn)
    st_shape = jax.ShapeDtypeStruct((b, nc, SSD_GROUPS, SSD_STATE, GROUP_WIDTH), BF16)
    st_block = (None, None, SSD_GROUPS, SSD_STATE, GROUP_WIDTH)
    return pl.pallas_call(
        _state_kernel,
        grid=(b, nc),
        in_specs=[pl.BlockSpec((None, CHUNK, d_inner), fwd),
                  pl.BlockSpec((None, CHUNK, gn), fwd_b),
                  pl.BlockSpec((None, CHUNK, nh2), fwd),
                  pl.BlockSpec((None, CHUNK, nh2), fwd),
                  pl.BlockSpec((None, CHUNK, d_inner), bwd),
                  pl.BlockSpec((None, CHUNK, gn), bwd_b),
                  pl.BlockSpec((None, CHUNK, nh2), bwd),
                  pl.BlockSpec((None, CHUNK, nh2), bwd),
                  pl.BlockSpec(expand.shape, lambda bi, t: (0, 0, 0))],
        out_specs=[pl.BlockSpec(st_block, lambda bi, t: (bi, t, 0, 0, 0)),
                   pl.BlockSpec(st_block, lambda bi, t: (bi, nc - 1 - t, 0, 0, 0))],
        out_shape=[st_shape, st_shape],
        scratch_shapes=[pltpu.VMEM((SSD_GROUPS, SSD_STATE, GROUP_WIDTH), F32),
                        pltpu.VMEM((SSD_GROUPS, SSD_STATE, GROUP_WIDTH), F32)],
        compiler_params=_params(("parallel", "arbitrary"), 40),
        name="ssd_states",
    )(xbc, xbc, dt, cs, xbc, xbc, dt, cs, expand)


def _ssd_out_kernel(x_ref, b_ref, c_ref, z_ref, dt_ref, cs_ref, colg_ref, rowg_ref, hf_ref, hb_ref,
                    e_ref, dskip_ref, nw_ref, o_ref):
    hpg = HEADS_PER_GROUP
    half = CHUNK // 2
    lower = (lax.broadcasted_iota(jnp.int32, (half, half), 0)
             >= lax.broadcasted_iota(jnp.int32, (half, half), 1))
    lane = lax.broadcasted_iota(jnp.int32, (CHUNK, 2 * SSD_HEAD_DIM), 1)
    zero_bf = jnp.zeros((CHUNK, 2 * SSD_HEAD_DIM), BF16)
    edec = jnp.exp2(cs_ref[...]).astype(BF16)
    dt_bf = dt_ref[...].astype(BF16)

    def one_group(g, carry):
        wide = pl.ds(pl.multiple_of(g * GROUP_WIDTH, GROUP_WIDTH), GROUP_WIDTH)
        narrow = pl.ds(pl.multiple_of(g * SSD_STATE, SSD_STATE), SSD_STATE)
        x = x_ref[:, wide]
        cm = c_ref[:, narrow]
        bm = b_ref[:, narrow]
        cb = lax.dot_general(cm, bm, (((1,), (1,)), ((), ())),
                             preferred_element_type=F32).astype(BF16)
        colg = colg_ref[g]
        rowg = rowg_ref[g]
        ys = []
        for p in range(hpg // 2):
            gms = []
            for hh in (2 * p, 2 * p + 1):
                cf = colg[:, hh:hh + 1]
                sb = colg[:, hpg + hh:hpg + hh + 1]
                rf = rowg[hh:hh + 1, :]
                rb = rowg[hpg + hh:hpg + hh + 1, :]
                top = jnp.concatenate(
                    [jnp.where(lower, cf[:half] - rf[:, :half], sb[:half] - rb[:, :half]),
                     sb[:half] - rb[:, half:]], axis=1)
                bot = jnp.concatenate(
                    [cf[half:] - rf[:, :half],
                     jnp.where(lower, cf[half:] - rf[:, half:], sb[half:] - rb[:, half:])], axis=1)
                arg = jnp.concatenate([top, bot], axis=0)
                gms.append(jnp.exp2(arg).astype(BF16) * cb)
            g2 = jnp.concatenate(gms, axis=1)
            xp = x[:, p * 2 * SSD_HEAD_DIM:(p + 1) * 2 * SSD_HEAD_DIM]
            xbd = jnp.concatenate([jnp.where(lane < SSD_HEAD_DIM, xp, zero_bf),
                                   jnp.where(lane >= SSD_HEAD_DIM, xp, zero_bf)], axis=0)
            ys.append(jnp.dot(g2, xbd, preferred_element_type=F32))
        y = jnp.concatenate(ys, axis=1)

        sel_f = e_ref[0, :, wide]
        sel_b = e_ref[1, :, wide]
        diag_cb = jnp.sum(cm.astype(F32) * bm.astype(F32), axis=-1, keepdims=True)
        dt_b = jnp.dot(dt_bf, sel_b, preferred_element_type=F32)
        y = y + x.astype(F32) * (diag_cb * dt_b + dskip_ref[:, wide])
        ef = jnp.dot(edec, sel_f, preferred_element_type=F32)
        eb = jnp.dot(edec, sel_b, preferred_element_type=F32)
        y = y + jnp.dot(cm, hf_ref[g], preferred_element_type=F32) * ef
        y = y + jnp.dot(cm, hb_ref[g], preferred_element_type=F32) * eb

        z = z_ref[:, wide].astype(F32)
        gz = y * (z * jax.nn.sigmoid(z))
        ms = jnp.mean(gz * gz, axis=-1, keepdims=True)
        o_ref[:, wide] = (gz * lax.rsqrt(ms + EPS) * nw_ref[:, wide]).astype(o_ref.dtype)
        return carry

    lax.fori_loop(0, SSD_GROUPS, one_group, 0)


def _ssd_out(zxbc, xbc, dt, cs, colg, rowg, hf, hb, expand, d_skip, norm_w, d_inner):
    b, lp, _ = xbc.shape
    nc = lp // CHUNK
    w = 2 * HEADS_PER_GROUP
    gn = SSD_GROUPS * SSD_STATE
    assert d_inner == SSD_GROUPS * GROUP_WIDTH and d_inner % gn == 0 and xbc.shape[2] == d_inner + 2 * gn
    st_block = (None, None, SSD_GROUPS, SSD_STATE, GROUP_WIDTH)
    chunk_rows = lambda bi, c: (bi, c, 0)
    per_chunk = lambda bi, c: (bi, c, 0, 0, 0)
    return pl.pallas_call(
        _ssd_out_kernel,
        grid=(b, nc),
        in_specs=[
            pl.BlockSpec((None, CHUNK, d_inner), chunk_rows),
            pl.BlockSpec((None, CHUNK, gn), lambda bi, c: (bi, c, d_inner // gn)),
            pl.BlockSpec((None, CHUNK, gn), lambda bi, c: (bi, c, d_inner // gn + 1)),
            pl.BlockSpec((None, CHUNK, d_inner), chunk_rows),
            pl.BlockSpec((None, CHUNK, dt.shape[2]), chunk_rows),
            pl.BlockSpec((None, CHUNK, cs.shape[2]), chunk_rows),
            pl.BlockSpec((None, None, SSD_GROUPS, CHUNK, w), per_chunk),
            pl.BlockSpec((None, None, SSD_GROUPS, w, CHUNK), per_chunk),
            pl.BlockSpec(st_block, per_chunk),
            pl.BlockSpec(st_block, per_chunk),
            pl.BlockSpec(expand.shape, lambda bi, c: (0, 0, 0)),
            pl.BlockSpec((1, d_inner), lambda bi, c: (0, 0)),
            pl.BlockSpec((1, d_inner), lambda bi, c: (0, 0)),
        ],
        out_specs=pl.BlockSpec((None, CHUNK, d_inner), chunk_rows),
        out_shape=jax.ShapeDtypeStruct((b, lp, d_inner), BF16),
        compiler_params=_params(("parallel", "parallel"), 40),
        name="ssd_out",
    )(xbc, xbc, xbc, zxbc, dt, cs, colg, rowg, hf, hb, expand,
      d_skip.reshape(1, d_inner).astype(F32), norm_w.reshape(1, d_inner).astype(F32))


def _group_major_dt_perm(n_heads):
    perm = []
    for g in range(SSD_GROUPS):
        for direction in range(2):
            for e in range(HEADS_PER_GROUP):
                perm.append(direction * n_heads + g * HEADS_PER_GROUP + e)
    return jnp.array(perm, dtype=jnp.int32)


def _ssd_layer(h, l_tok, layer, norm_w, w_in_stack, conv_w, conv_b, dt_bias, a_log, d_skip,
               gnorm_w, w_out_stack):
    b, lp, d = h.shape
    d_inner = w_out_stack.shape[1]
    n_heads = d_inner // SSD_HEAD_DIM
    n_main = d_inner + conv_w.shape[1]
    perm = _group_major_dt_perm(n_heads)
    h2d = h.reshape(b * lp, d)
    u = _rmsnorm(h2d, norm_w, BF16)
    zxbc = _matmul(u, w_in_stack, layer, n_main, BF16, 1024, 1024).reshape(b, lp, n_main)
    w_dt = jnp.take(w_in_stack[layer, :, n_main:], perm, axis=1)[None]
    dt_raw = _matmul(u, w_dt, 0, 2 * n_heads, F32, 1024, 2 * n_heads).reshape(b, lp, 2 * n_heads)
    xbc = _conv_silu(zxbc, conv_w, conv_b, l_tok, d_inner)
    dt, cs, colg, rowg = _dt_prepare(dt_raw, jnp.take(dt_bias.reshape(-1), perm),
                                     jnp.take(a_log.reshape(-1), perm), l_tok)
    expand = _head_expand_table()
    hf, hb = _ssd_states(xbc, dt, cs, expand, d_inner)
    yn = _ssd_out(zxbc, xbc, dt, cs, colg, rowg, hf, hb, expand,
                  jnp.repeat(d_skip, SSD_HEAD_DIM), gnorm_w, d_inner)
    h2 = _matmul(yn.reshape(b * lp, d_inner), w_out_stack, layer, d, F32, 512, 1024, res=h2d,
                 w_buffers=1, vmem_mb=52)
    return h2.reshape(b, lp, d)


def _ffn_layer(h, layer, norm_w, w_gate_stack, w_up_stack, w_down_stack, u=None):
    b, lp, d = h.shape
    h2d = h.reshape(b * lp, d)
    if u is None:
        u = _rmsnorm(h2d, norm_w, BF16)
    act = _swiglu(u, w_gate_stack, w_up_stack, layer)
    h2 = _matmul(act, w_down_stack, layer, d, F32, 256, 1024, res=h2d, w_buffers=1, vmem_mb=56)
    return h2.reshape(b, lp, d)


def _trunk(x, meta_tokens, norm_mix_w, norm_ffn_w, norm_final_w, fnet_w_out,
           ssd_w_in, ssd_conv_w, ssd_conv_b, ssd_dt_bias, ssd_a_log, ssd_d,
           ssd_norm_w, ssd_w_out, ffn_w_gate, ffn_w_up, ffn_w_down):
    b, seq, d = x.shape
    n_meta = meta_tokens.shape[0]
    l_tok = n_meta + seq
    lp = -(-l_tok // ROW_PAD) * ROW_PAD
    depth = norm_mix_w.shape[0]
    meta = jnp.broadcast_to(meta_tokens.astype(x.dtype)[None], (b, n_meta, d))
    h = jnp.concatenate([meta, x, jnp.zeros((b, lp - l_tok, d), x.dtype)], axis=1)
    hp = -(-(l_tok // 2 + 1) // FNET_TM) * FNET_TM
    chan_tabs = _chan_dft_tables()
    seq_tabs = _seq_dft_tables(l_tok, hp)
    for i in range(depth):
        j = i // 2
        u = None
        if i % 2 == 0:
            h, u = _fourier_layer(h, l_tok, norm_mix_w[i], fnet_w_out, j, norm_ffn_w[i],
                                  chan_tabs, seq_tabs)
        else:
            h = _ssd_layer(h, l_tok, j, norm_mix_w[i], ssd_w_in, ssd_conv_w[j], ssd_conv_b[j],
                           ssd_dt_bias[j], ssd_a_log[j], ssd_d[j], ssd_norm_w[j], ssd_w_out)
        h = _ffn_layer(h, i, norm_ffn_w[i], ffn_w_gate, ffn_w_up, ffn_w_down, u=u)
    return _final_norm(h, norm_final_w, n_meta, seq)


def kernel(x, meta_tokens, norm_mix_w, norm_ffn_w, norm_final_w, fnet_w_out, ssd_w_in, ssd_conv_w, ssd_conv_b, ssd_dt_bias, ssd_a_log, ssd_d, ssd_norm_w, ssd_w_out, ffn_w_gate, ffn_w_up, ffn_w_down):
    return _trunk(x, meta_tokens, norm_mix_w, norm_ffn_w, norm_final_w, fnet_w_out,
                  ssd_w_in, ssd_conv_w, ssd_conv_b, ssd_dt_bias, ssd_a_log, ssd_d,
                  ssd_norm_w, ssd_w_out, ffn_w_gate, ffn_w_up, ffn_w_down)
```

```python
import functools
import math

import jax
import jax.numpy as jnp
from jax import lax
from jax.experimental import pallas as pl
from jax.experimental.pallas import tpu as pltpu

F32 = jnp.float32
BF16 = jnp.bfloat16

N_META = 16
FNET_GROUP_WIDTH = 256
SSD_HEAD_DIM = 64
SSD_GROUPS = 8
SSD_STATE = 128
HEADS_PER_GROUP = 8
GROUP_WIDTH = HEADS_PER_GROUP * SSD_HEAD_DIM
CONV_WIDTH = 5
CHUNK = 256
LOG2E = 1.4426950408889634
HALO = 16
EPS = 1e-6
ROW_PAD = 256


def _params(semantics, vmem_mb):
    return pltpu.CompilerParams(dimension_semantics=semantics,
                                vmem_limit_bytes=vmem_mb * 1024 * 1024)


def _rmsnorm_kernel(h_ref, w_ref, o_ref):
    x = h_ref[...]
    ms = jnp.mean(x * x, axis=-1, keepdims=True)
    o_ref[...] = (x * lax.rsqrt(ms + EPS) * w_ref[...]).astype(o_ref.dtype)


def _rmsnorm(h2d, w, out_dtype, tm=512):
    m, d = h2d.shape
    assert m % tm == 0, (m, tm)
    return pl.pallas_call(
        _rmsnorm_kernel,
        grid=(m // tm,),
        in_specs=[pl.BlockSpec((tm, d), lambda i: (i, 0)),
                  pl.BlockSpec((1, d), lambda i: (0, 0))],
        out_specs=pl.BlockSpec((tm, d), lambda i: (i, 0)),
        out_shape=jax.ShapeDtypeStruct((m, d), out_dtype),
        compiler_params=_params(("parallel",), 40),
        name="rmsnorm",
    )(h2d, w.reshape(1, d).astype(F32))


def _final_norm_kernel(h_ref, nxt_ref, w_ref, o_ref, *, shift):
    def norm(x):
        ms = jnp.mean(x * x, axis=-1, keepdims=True)
        return x * lax.rsqrt(ms + EPS) * w_ref[...]

    tm = o_ref.shape[0]
    o_ref[0:tm - shift, :] = norm(h_ref[shift:, :]).astype(o_ref.dtype)
    o_ref[tm - shift:, :] = norm(nxt_ref[...]).astype(o_ref.dtype)


def _final_norm(h, w, n_skip, seq, tm=256):
    b, lp, d = h.shape
    assert seq % tm == 0 and tm % n_skip == 0 and n_skip % SUBLANES == 0 and lp >= seq + tm, (seq, n_skip, lp)
    per = tm // n_skip
    return pl.pallas_call(
        functools.partial(_final_norm_kernel, shift=n_skip),
        grid=(b, seq // tm),
        in_specs=[pl.BlockSpec((None, tm, d), lambda bi, i: (bi, i, 0)),
                  pl.BlockSpec((None, n_skip, d), lambda bi, i: (bi, (i + 1) * per, 0)),
                  pl.BlockSpec((1, d), lambda bi, i: (0, 0))],
        out_specs=pl.BlockSpec((None, tm, d), lambda bi, i: (bi, i, 0)),
        out_shape=jax.ShapeDtypeStruct((b, seq, d), h.dtype),
        compiler_params=_params(("parallel", "parallel"), 40),
        name="final_norm",
    )(h, h, w.reshape(1, d).astype(F32))


def _mm_kernel(a_ref, w_ref, *rest, has_res):
    if has_res:
        r_ref, o_ref, wbf = rest
    else:
        o_ref, wbf = rest

    @pl.when(pl.program_id(1) == 0)
    def _():
        wbf[...] = w_ref[...].astype(BF16)

    acc = jnp.dot(a_ref[...], wbf[...], preferred_element_type=F32)
    if has_res:
        acc = r_ref[...] + acc
    o_ref[...] = acc.astype(o_ref.dtype)


def _matmul(a, w_stack, layer, n, out_dtype, tm, tn, col0=0, res=None, w_buffers=2, vmem_mb=48):
    m, k = a.shape
    assert w_stack.shape[1] == k, (w_stack.shape, k)
    assert m % tm == 0 and n % tn == 0 and col0 % tn == 0, (m, n, tm, tn, col0)
    cb = col0 // tn
    w_kwargs = {} if w_buffers == 2 else {"pipeline_mode": pl.Buffered(w_buffers)}
    in_specs = [pl.BlockSpec((tm, k), lambda j, i: (i, 0)),
                pl.BlockSpec((None, k, tn), lambda j, i: (layer, 0, cb + j), **w_kwargs)]
    args = [a, w_stack]
    aliases = {}
    if res is not None:
        in_specs.append(pl.BlockSpec((tm, tn), lambda j, i: (i, j)))
        args.append(res)
        aliases = {2: 0}
    return pl.pallas_call(
        functools.partial(_mm_kernel, has_res=res is not None),
        grid=(n // tn, m // tm),
        in_specs=in_specs,
        out_specs=pl.BlockSpec((tm, tn), lambda j, i: (i, j)),
        out_shape=jax.ShapeDtypeStruct((m, n), out_dtype),
        scratch_shapes=[pltpu.VMEM((k, tn), BF16)],
        input_output_aliases=aliases,
        compiler_params=_params(("parallel", "arbitrary"), vmem_mb),
        name="matmul_res" if res is not None else "matmul",
    )(*args)


def _swiglu_kernel(a_ref, wg_ref, wu_ref, o_ref, wg_bf, wu_bf):
    @pl.when(pl.program_id(1) == 0)
    def _():
        wg_bf[...] = wg_ref[...].astype(BF16)
        wu_bf[...] = wu_ref[...].astype(BF16)

    a = a_ref[...]
    gate = jnp.dot(a, wg_bf[...], preferred_element_type=F32)
    up = jnp.dot(a, wu_bf[...], preferred_element_type=F32)
    o_ref[...] = (gate * jax.nn.sigmoid(gate) * up).astype(o_ref.dtype)


def _swiglu(a, wg_stack, wu_stack, layer, tm=1024, tn=512):
    m, k = a.shape
    n = wg_stack.shape[2]
    assert m % tm == 0 and n % tn == 0, (m, n, tm, tn)
    w_spec = pl.BlockSpec((None, k, tn), lambda j, i: (layer, 0, j))
    return pl.pallas_call(
        _swiglu_kernel,
        grid=(n // tn, m // tm),
        in_specs=[pl.BlockSpec((tm, k), lambda j, i: (i, 0)), w_spec, w_spec],
        out_specs=pl.BlockSpec((tm, tn), lambda j, i: (i, j)),
        out_shape=jax.ShapeDtypeStruct((m, n), BF16),
        scratch_shapes=[pltpu.VMEM((k, tn), BF16), pltpu.VMEM((k, tn), BF16)],
        compiler_params=_params(("parallel", "arbitrary"), 48),
        name="swiglu",
    )(a, wg_stack, wu_stack)


FNET_TM = 256


def _mirror_tile(l_tok, i):
    return (l_tok - (i + 1) * FNET_TM) // FNET_TM


def _mirror_select(l_tok):
    r = lax.broadcasted_iota(jnp.int32, (FNET_TM, 2 * FNET_TM), 0)
    c = lax.broadcasted_iota(jnp.int32, (FNET_TM, 2 * FNET_TM), 1)
    return jnp.where(c == (l_tok % FNET_TM) + FNET_TM - r, 1.0, 0.0).astype(BF16)


def _fnet_chan_kernel(h_ref, ha_ref, hb_ref, w_ref, c_ref, s_ref, o_ref, *, l_tok):
    def norm(ref):
        x = ref[...]
        ms = jnp.mean(x * x, axis=-1, keepdims=True)
        return (x * lax.rsqrt(ms + EPS) * w_ref[...]).astype(BF16)

    u = norm(h_ref).astype(F32)
    mirror_src = jnp.concatenate([norm(ha_ref), norm(hb_ref)], axis=0)
    um = jnp.dot(_mirror_select(l_tok), mirror_src, preferred_element_type=F32)
    us = (u + um).astype(BF16)
    ud = (u - um).astype(BF16)
    gw = FNET_GROUP_WIDTH
    for g in range(us.shape[1] // gw):
        cols = slice(g * gw, (g + 1) * gw)
        o_ref[0, :, cols] = jnp.dot(us[:, cols], c_ref[...], preferred_element_type=F32).astype(BF16)
        o_ref[1, :, cols] = jnp.dot(ud[:, cols], s_ref[...], preferred_element_type=F32).astype(BF16)


def _fnet_chan(h, w, c_tab, s_tab, l_tok, hp):
    b, lp, d = h.shape
    tm = FNET_TM
    nh = hp // tm
    assert lp % tm == 0 and hp % tm == 0 and d % FNET_GROUP_WIDTH == 0, (lp, hp, d)
    assert l_tok % 2 == 0 and lp > l_tok and l_tok >= nh * tm and hp > l_tok // 2, (l_tok, lp, hp)
    blk = (None, tm, d)
    return pl.pallas_call(
        functools.partial(_fnet_chan_kernel, l_tok=l_tok),
        grid=(b, nh),
        in_specs=[pl.BlockSpec(blk, lambda bi, i: (bi, i, 0)),
                  pl.BlockSpec(blk, lambda bi, i: (bi, _mirror_tile(l_tok, i), 0)),
                  pl.BlockSpec(blk, lambda bi, i: (bi, _mirror_tile(l_tok, i) + 1, 0)),
                  pl.BlockSpec((1, d), lambda bi, i: (0, 0)),
                  pl.BlockSpec(c_tab.shape, lambda bi, i: (0, 0)),
                  pl.BlockSpec(s_tab.shape, lambda bi, i: (0, 0))],
        out_specs=pl.BlockSpec((None, 2, tm, d), lambda bi, i: (bi, 0, i, 0)),
        out_shape=jax.ShapeDtypeStruct((b, 2, hp, d), BF16),
        compiler_params=_params(("parallel", "parallel"), 48),
        name="fnet_chan",
    )(h, h, h, w.reshape(1, d).astype(F32), c_tab, s_tab)


def _seq_dft_kernel(wc_ref, ws_ref, us_ref, ud_ref, o_ref):
    p = jnp.dot(wc_ref[...], us_ref[...], preferred_element_type=F32)
    q = jnp.dot(ws_ref[...], ud_ref[...], preferred_element_type=F32)
    o_ref[0] = (p - q).astype(o_ref.dtype)
    o_ref[1] = (p + q).astype(o_ref.dtype)


def _seq_dft(wc, ws, ab, tn=512):
    b, _, hp, d = ab.shape
    tm = next(t for t in (768, 512, 256) if hp % t == 0)
    assert d % tn == 0 and wc.shape == (hp, hp) and ws.shape == (hp, hp), (d, tn, wc.shape)
    return pl.pallas_call(
        _seq_dft_kernel,
        grid=(b, d // tn, hp // tm),
        in_specs=[pl.BlockSpec((tm, hp), lambda bi, j, i: (i, 0)),
                  pl.BlockSpec((tm, hp), lambda bi, j, i: (i, 0)),
                  pl.BlockSpec((None, None, hp, tn), lambda bi, j, i: (bi, 0, 0, j)),
                  pl.BlockSpec((None, None, hp, tn), lambda bi, j, i: (bi, 1, 0, j))],
        out_specs=pl.BlockSpec((None, 2, tm, tn), lambda bi, j, i: (bi, 0, i, j)),
        out_shape=jax.ShapeDtypeStruct((b, 2, hp, d), BF16),
        compiler_params=_params(("parallel", "parallel", "parallel"), 48),
        name="seq_dft",
    )(wc, ws, ab, ab)


def _fnet_out_kernel(fd_ref, fa_ref, fb_ref, w_ref, h_ref, nw_ref, o_ref, u_ref, wbf, *,
                     l_tok, tiles_per_seq):
    t = pl.program_id(1)

    @pl.when(t == 0)
    def _():
        wbf[...] = w_ref[...].astype(BF16)

    tm = FNET_TM
    mirror_src = jnp.concatenate([fa_ref[...], fb_ref[...]], axis=0)
    fm = jnp.dot(_mirror_select(l_tok), mirror_src, preferred_element_type=F32)
    row = (t % tiles_per_seq) * tm + lax.broadcasted_iota(jnp.int32, (tm, 1), 0)
    f = jnp.where(row <= l_tok // 2, fd_ref[...].astype(F32), fm)
    f = jnp.where(row < l_tok, f, 0.0).astype(BF16)
    hn = h_ref[...] + jnp.dot(f, wbf[...], preferred_element_type=F32)
    o_ref[...] = hn
    ms = jnp.mean(hn * hn, axis=-1, keepdims=True)
    u_ref[...] = (hn * lax.rsqrt(ms + EPS) * nw_ref[...]).astype(u_ref.dtype)


def _fnet_out(fm, w_stack, layer, h2d, next_norm_w, l_tok, lp):
    b, _, hp, d = fm.shape
    tm = FNET_TM
    nh = hp // tm
    tps = lp // tm
    n = w_stack.shape[2]
    tn = n
    assert h2d.shape == (b * lp, n), (n, h2d.shape)
    clip = lambda v: jnp.clip(v, 0, nh - 1)
    blk = (None, None, tm, d)
    return pl.pallas_call(
        functools.partial(_fnet_out_kernel, l_tok=l_tok, tiles_per_seq=tps),
        grid=(n // tn, b * tps),
        in_specs=[pl.BlockSpec(blk, lambda j, t: (t // tps, 0, clip(t % tps), 0)),
                  pl.BlockSpec(blk, lambda j, t: (t // tps, 1, clip(_mirror_tile(l_tok, t % tps)), 0)),
                  pl.BlockSpec(blk, lambda j, t: (t // tps, 1, clip(_mirror_tile(l_tok, t % tps) + 1), 0)),
                  pl.BlockSpec((None, d, tn), lambda j, t: (layer, 0, j), pipeline_mode=pl.Buffered(1)),
                  pl.BlockSpec((tm, tn), lambda j, t: (t, j)),
                  pl.BlockSpec((1, tn), lambda j, t: (0, j))],
        out_specs=[pl.BlockSpec((tm, tn), lambda j, t: (t, j)),
                   pl.BlockSpec((tm, tn), lambda j, t: (t, j))],
        out_shape=[jax.ShapeDtypeStruct(h2d.shape, F32), jax.ShapeDtypeStruct(h2d.shape, BF16)],
        scratch_shapes=[pltpu.VMEM((d, tn), BF16)],
        input_output_aliases={4: 0},
        compiler_params=_params(("parallel", "arbitrary"), 56),
        name="fnet_out",
    )(fm, fm, fm, w_stack, h2d, next_norm_w.reshape(1, n).astype(F32))


def _chan_dft_tables():
    n = FNET_GROUP_WIDTH
    j = jnp.arange(n, dtype=jnp.int32)
    th = ((j[:, None] * j[None, :]) % n).astype(F32) * (2.0 * math.pi / n)
    scale = 1.0 / math.sqrt(n)
    return (jnp.cos(th) * scale).astype(BF16), (jnp.sin(th) * scale).astype(BF16)


def _seq_dft_tables(l_tok, hp):
    blk = 64
    half = l_tok // 2
    k = jnp.arange(hp, dtype=jnp.int32)[:, None]
    a = jnp.arange(hp // blk, dtype=jnp.int32)[None, :] * blk
    b = jnp.arange(blk, dtype=jnp.int32)[None, :]
    w0 = 2.0 * math.pi / l_tok
    th1 = ((k * a) % l_tok).astype(F32) * w0
    th2 = ((k * b) % l_tok).astype(F32) * w0
    c1, s1 = jnp.cos(th1)[:, :, None], jnp.sin(th1)[:, :, None]
    c2, s2 = jnp.cos(th2)[:, None, :], jnp.sin(th2)[:, None, :]
    n = a[:, :, None] + b[:, None, :]
    valid = (k <= half)[:, :, None] & (n <= half)
    scale = 1.0 / math.sqrt(l_tok)
    col_w = jnp.where(n == half, 0.5 * scale, scale)
    wc = jnp.where(valid, (c1 * c2 - s1 * s2) * col_w, 0.0).reshape(hp, hp)
    ws = jnp.where(valid, (s1 * c2 + c1 * s2) * scale, 0.0).reshape(hp, hp)
    return wc.astype(BF16), ws.astype(BF16)


def _fourier_layer(h, l_tok, norm_w, w_out_stack, layer, next_norm_w, chan_tabs, seq_tabs):
    b, lp, d = h.shape
    hp = seq_tabs[0].shape[0]
    ab = _fnet_chan(h, norm_w, chan_tabs[0], chan_tabs[1], l_tok, hp)
    fm = _seq_dft(seq_tabs[0], seq_tabs[1], ab)
    h2, u = _fnet_out(fm, w_out_stack, layer, h.reshape(b * lp, d), next_norm_w, l_tok, lp)
    return h2.reshape(b, lp, d), u


def _split3(x):
    hi = x.astype(BF16)
    r = x - hi.astype(F32)
    mid = r.astype(BF16)
    lo = (r - mid.astype(F32)).astype(BF16)
    return hi, mid, lo


CONV_SUB = 512
SUBLANES = 8


def _conv_kernel(prev_ref, cur_ref, next_ref, shift_ref, w_ref, b_ref, o_ref, *, l_tok):
    i = pl.program_id(1)
    n = pl.num_programs(1)
    rows = cur_ref.shape[0]
    half = CONV_WIDTH // 2
    offsets = [d for d in range(-half, half + 1) if d != 0]
    row = i * rows + lax.broadcasted_iota(jnp.int32, (rows, CONV_SUB), 0)
    sub = lax.broadcasted_iota(jnp.int32, (SUBLANES, CONV_SUB), 0)
    for s in range(cur_ref.shape[1] // CONV_SUB):
        cols = slice(s * CONV_SUB, (s + 1) * CONV_SUB)
        xc = cur_ref[:, cols]
        shifted = jnp.dot(shift_ref[...], xc, preferred_element_type=F32)
        acc = xc.astype(F32) * w_ref[half:half + 1, cols] + b_ref[:, cols]
        for t, d in enumerate(offsets):
            acc = acc + shifted[t * rows:(t + 1) * rows] * w_ref[half + d:half + d + 1, cols]
        prev = jnp.where(i > 0, prev_ref[:, cols].astype(F32)[HALO - SUBLANES:, :], 0.0)
        nxt = jnp.where(i < n - 1, next_ref[:, cols].astype(F32)[:SUBLANES, :], 0.0)
        top = jnp.zeros((SUBLANES, CONV_SUB), F32)
        bot = jnp.zeros((SUBLANES, CONV_SUB), F32)
        for d in range(1, half + 1):
            top = top + jnp.where(sub < d, pltpu.roll(prev, d, 0), 0.0) * w_ref[half - d:half - d + 1, cols]
            bot = bot + jnp.where(sub >= SUBLANES - d, pltpu.roll(nxt, SUBLANES - d, 0), 0.0) \
                * w_ref[half + d:half + d + 1, cols]
        acc = jnp.concatenate([acc[:SUBLANES] + top, acc[SUBLANES:rows - SUBLANES],
                               acc[rows - SUBLANES:] + bot], axis=0)
        y = acc * jax.nn.sigmoid(acc)
        o_ref[:, cols] = jnp.where(row < l_tok, y, 0.0).astype(o_ref.dtype)


def _conv_shift_table(rows):
    half = CONV_WIDTH // 2
    r = jnp.arange(rows, dtype=jnp.int32)[:, None]
    c = jnp.arange(rows, dtype=jnp.int32)[None, :]
    blocks = [jnp.where(c == r + d, 1.0, 0.0) for d in range(-half, half + 1) if d != 0]
    return jnp.concatenate(blocks, axis=0).astype(BF16)


def _conv_silu(zxbc, conv_w, conv_b, l_tok, d_inner, tc=2048):
    b, lp, _ = zxbc.shape
    conv_dim = conv_w.shape[1]
    assert lp % CHUNK == 0 and conv_dim % tc == 0 and d_inner % tc == 0, (lp, conv_dim, d_inner)
    assert tc % CONV_SUB == 0 and HALO >= SUBLANES >= CONV_WIDTH // 2
    c0 = d_inner // tc
    hb = CHUNK // HALO
    nh = lp // HALO
    return pl.pallas_call(
        functools.partial(_conv_kernel, l_tok=l_tok),
        grid=(b, lp // CHUNK, conv_dim // tc),
        in_specs=[
            pl.BlockSpec((None, HALO, tc), lambda bi, i, j: (bi, jnp.maximum(i * hb - 1, 0), c0 + j)),
            pl.BlockSpec((None, CHUNK, tc), lambda bi, i, j: (bi, i, c0 + j)),
            pl.BlockSpec((None, HALO, tc), lambda bi, i, j: (bi, jnp.minimum((i + 1) * hb, nh - 1), c0 + j)),
            pl.BlockSpec(((CONV_WIDTH - 1) * CHUNK, CHUNK), lambda bi, i, j: (0, 0)),
            pl.BlockSpec((CONV_WIDTH, tc), lambda bi, i, j: (0, j)),
            pl.BlockSpec((1, tc), lambda bi, i, j: (0, j)),
        ],
        out_specs=pl.BlockSpec((None, CHUNK, tc), lambda bi, i, j: (bi, i, j)),
        out_shape=jax.ShapeDtypeStruct((b, lp, conv_dim), BF16),
        compiler_params=_params(("parallel", "parallel", "parallel"), 32),
        name="conv_silu",
    )(zxbc, zxbc, zxbc, _conv_shift_table(CHUNK), conv_w.astype(F32),
      conv_b.reshape(1, conv_dim).astype(F32))


def _dt_kernel(raw_ref, bias_ref, alog_ref, dt_ref, cs_ref, colg_ref, rowg_ref, *, l_tok):
    c = pl.program_id(1)
    x = raw_ref[...] + bias_ref[...]
    dt = jnp.maximum(x, 0.0) + jnp.log1p(jnp.exp(-jnp.abs(x)))
    row = c * CHUNK + lax.broadcasted_iota(jnp.int32, x.shape, 0)
    dt = jnp.where(row < l_tok, dt, 0.0)
    da = dt * (-jnp.exp(alog_ref[...]))
    li = lax.broadcasted_iota(jnp.int32, (CHUNK, CHUNK), 0)
    si = lax.broadcasted_iota(jnp.int32, (CHUNK, CHUNK), 1)
    tri_l = jnp.where(li >= si, 1.0, 0.0).astype(BF16)
    tri_u = jnp.where(li <= si, 1.0, 0.0).astype(BF16)
    hi, mid, lo = _split3(da)
    prefix = (jnp.dot(tri_l, hi, preferred_element_type=F32)
              + jnp.dot(tri_l, mid, preferred_element_type=F32)
              + jnp.dot(tri_l, lo, preferred_element_type=F32))
    suffix = (jnp.dot(tri_u, hi, preferred_element_type=F32)
              + jnp.dot(tri_u, mid, preferred_element_type=F32)
              + jnp.dot(tri_u, lo, preferred_element_type=F32))
    col = lax.broadcasted_iota(jnp.int32, x.shape, 1)
    cs = jnp.where((col & HEADS_PER_GROUP) != 0, suffix, prefix) * LOG2E
    dt_ref[...] = dt
    cs_ref[...] = cs
    src_t = (cs - jnp.log2(dt)).T
    w = 2 * HEADS_PER_GROUP
    for g in range(SSD_GROUPS):
        colg_ref[g] = cs[:, g * w:(g + 1) * w]
        rowg_ref[g] = src_t[g * w:(g + 1) * w, :]


def _dt_prepare(dt_raw, dt_bias, a_log, l_tok):
    b, lp, nh2 = dt_raw.shape
    assert lp % CHUNK == 0 and nh2 == 2 * HEADS_PER_GROUP * SSD_GROUPS, (lp, nh2)
    nc = lp // CHUNK
    w = 2 * HEADS_PER_GROUP
    blk = lambda bi, c: (bi, c, 0)
    return pl.pallas_call(
        functools.partial(_dt_kernel, l_tok=l_tok),
        grid=(b, nc),
        in_specs=[pl.BlockSpec((None, CHUNK, nh2), blk),
                  pl.BlockSpec((1, nh2), lambda bi, c: (0, 0)),
                  pl.BlockSpec((1, nh2), lambda bi, c: (0, 0))],
        out_specs=[pl.BlockSpec((None, CHUNK, nh2), blk),
                   pl.BlockSpec((None, CHUNK, nh2), blk),
                   pl.BlockSpec((None, None, SSD_GROUPS, CHUNK, w), lambda bi, c: (bi, c, 0, 0, 0)),
                   pl.BlockSpec((None, None, SSD_GROUPS, w, CHUNK), lambda bi, c: (bi, c, 0, 0, 0))],
        out_shape=[jax.ShapeDtypeStruct((b, lp, nh2), F32),
                   jax.ShapeDtypeStruct((b, lp, nh2), F32),
                   jax.ShapeDtypeStruct((b, nc, SSD_GROUPS, CHUNK, w), F32),
                   jax.ShapeDtypeStruct((b, nc, SSD_GROUPS, w, CHUNK), F32)],
        compiler_params=_params(("parallel", "parallel"), 32),
        name="dt_prepare",
    )(dt_raw, dt_bias.reshape(1, nh2).astype(F32), a_log.reshape(1, nh2).astype(F32))


def _state_kernel(xf_ref, bf_ref, dtf_ref, csf_ref, xb_ref, bb_ref, dtb_ref, csb_ref, e_ref,
                  hf_out, hb_out, hf, hb):
    t = pl.program_id(1)

    @pl.when(t == 0)
    def _():
        hf[...] = jnp.zeros_like(hf)
        hb[...] = jnp.zeros_like(hb)

    lane8 = lax.broadcasted_iota(jnp.int32, (SUBLANES, SSD_STATE), 1)

    def one_direction(x_ref, b_ref, dt_ref, cs_ref, h_out, h, total_row, d):
        cs = cs_ref[...]
        total = cs[total_row:total_row + 1, :]
        col = lax.broadcasted_iota(jnp.int32, cs.shape, 1)
        mine = (col & HEADS_PER_GROUP) == d * HEADS_PER_GROUP
        scale = (dt_ref[...] * jnp.exp2(jnp.where(mine, total - cs, 0.0))).astype(BF16)
        decay8 = jnp.broadcast_to(jnp.exp2(total), (SUBLANES, cs.shape[1]))
        xdec = x_ref[...] * jnp.dot(scale, e_ref[d], preferred_element_type=F32).astype(BF16)
        for g in range(SSD_GROUPS):
            s_new = lax.dot_general(b_ref[:, g * SSD_STATE:(g + 1) * SSD_STATE],
                                    xdec[:, g * GROUP_WIDTH:(g + 1) * GROUP_WIDTH],
                                    (((0,), (0,)), ((), ())), preferred_element_type=F32)
            first = (2 * g + d) * HEADS_PER_GROUP
            dec = jnp.concatenate(
                [jnp.take_along_axis(decay8, first + 2 * j + (lane8 >> 6), axis=1)
                 for j in range(GROUP_WIDTH // SSD_STATE)], axis=1)[0:1, :]
            h_prev = h[g]
            h_out[g] = h_prev.astype(h_out.dtype)
            h[g] = h_prev * dec + s_new

    one_direction(xf_ref, bf_ref, dtf_ref, csf_ref, hf_out, hf, CHUNK - 1, 0)
    one_direction(xb_ref, bb_ref, dtb_ref, csb_ref, hb_out, hb, 0, 1)


def _head_expand_table():
    d = jnp.arange(2, dtype=jnp.int32)[:, None, None]
    r = jnp.arange(2 * SSD_GROUPS * HEADS_PER_GROUP, dtype=jnp.int32)[None, :, None]
    c = jnp.arange(SSD_GROUPS * GROUP_WIDTH, dtype=jnp.int32)[None, None, :]
    src = (2 * (c // GROUP_WIDTH) + d) * HEADS_PER_GROUP + (c % GROUP_WIDTH) // SSD_HEAD_DIM
    return jnp.where(r == src, 1.0, 0.0).astype(BF16)


def _ssd_states(xbc, dt, cs, expand, d_inner):
    b, lp, _ = xbc.shape
    nc = lp // CHUNK
    gn = SSD_GROUPS * SSD_STATE
    nh2 = dt.shape[2]
    fwd = lambda bi, t: (bi, t, 0)
    bwd = lambda bi, t: (bi, nc - 1 - t, 0)
    fwd_b = lambda bi, t: (bi, t, d_inner // gn)
    bwd_b = lambda bi, t: (bi, nc - 1 - t, d_inner // gn)
    st_shape = jax.ShapeDtypeStruct((b, nc, SSD_GROUPS, SSD_STATE, GROUP_WIDTH), BF16)
    st_block = (None, None, SSD_GROUPS, SSD_STATE, GROUP_WIDTH)
    return pl.pallas_call(
        _state_kernel,
        grid=(b, nc),
        in_specs=[pl.BlockSpec((None, CHUNK, d_inner), fwd),
                  pl.BlockSpec((None, CHUNK, gn), fwd_b),
                  pl.BlockSpec((None, CHUNK, nh2), fwd),
                  pl.BlockSpec((None, CHUNK, nh2), fwd),
                  pl.BlockSpec((None, CHUNK, d_inner), bwd),
                  pl.BlockSpec((None, CHUNK, gn), bwd_b),
                  pl.BlockSpec((None, CHUNK, nh2), bwd),
                  pl.BlockSpec((None, CHUNK, nh2), bwd),
                  pl.BlockSpec(expand.shape, lambda bi, t: (0, 0, 0))],
        out_specs=[pl.BlockSpec(st_block, lambda bi, t: (bi, t, 0, 0, 0)),
                   pl.BlockSpec(st_block, lambda bi, t: (bi, nc - 1 - t, 0, 0, 0))],
        out_shape=[st_shape, st_shape],
        scratch_shapes=[pltpu.VMEM((SSD_GROUPS, SSD_STATE, GROUP_WIDTH), F32),
                        pltpu.VMEM((SSD_GROUPS, SSD_STATE, GROUP_WIDTH), F32)],
        compiler_params=_params(("parallel", "arbitrary"), 40),
        name="ssd_states",
    )(xbc, xbc, dt, cs, xbc, xbc, dt, cs, expand)


def _ssd_out_kernel(x_ref, b_ref, c_ref, z_ref, dt_ref, cs_ref, colg_ref, rowg_ref, hf_ref, hb_ref,
                    e_ref, dskip_ref, nw_ref, o_ref):
    hpg = HEADS_PER_GROUP
    half = CHUNK // 2
    lower = (lax.broadcasted_iota(jnp.int32, (half, half), 0)
             >= lax.broadcasted_iota(jnp.int32, (half, half), 1))
    lane = lax.broadcasted_iota(jnp.int32, (CHUNK, 2 * SSD_HEAD_DIM), 1)
    zero_bf = jnp.zeros((CHUNK, 2 * SSD_HEAD_DIM), BF16)
    edec = jnp.exp2(cs_ref[...]).astype(BF16)
    dt_bf = dt_ref[...].astype(BF16)

    def one_group(g, carry):
        wide = pl.ds(pl.multiple_of(g * GROUP_WIDTH, GROUP_WIDTH), GROUP_WIDTH)
        narrow = pl.ds(pl.multiple_of(g * SSD_STATE, SSD_STATE), SSD_STATE)
        x = x_ref[:, wide]
        cm = c_ref[:, narrow]
        bm = b_ref[:, narrow]
        cb = lax.dot_general(cm, bm, (((1,), (1,)), ((), ())),
                             preferred_element_type=F32).astype(BF16)
        colg = colg_ref[g]
        rowg = rowg_ref[g]
        ys = []
        for p in range(hpg // 2):
            gms = []
            for hh in (2 * p, 2 * p + 1):
                cf = colg[:, hh:hh + 1]
                sb = colg[:, hpg + hh:hpg + hh + 1]
                rf = rowg[hh:hh + 1, :]
                rb = rowg[hpg + hh:hpg + hh + 1, :]
                top = jnp.concatenate(
                    [jnp.where(lower, cf[:half] - rf[:, :half], sb[:half] - rb[:, :half]),
                     sb[:half] - rb[:, half:]], axis=1)
                bot = jnp.concatenate(
                    [cf[half:] - rf[:, :half],
                     jnp.where(lower, cf[half:] - rf[:, half:], sb[half:] - rb[:, half:])], axis=1)
                arg = jnp.concatenate([top, bot], axis=0)
                gms.append(jnp.exp2(arg).astype(BF16) * cb)
            g2 = jnp.concatenate(gms, axis=1)
            xp = x[:, p * 2 * SSD_HEAD_DIM:(p + 1) * 2 * SSD_HEAD_DIM]
            xbd = jnp.concatenate([jnp.where(lane < SSD_HEAD_DIM, xp, zero_bf),
                                   jnp.where(lane >= SSD_HEAD_DIM, xp, zero_bf)], axis=0)
            ys.append(jnp.dot(g2, xbd, preferred_element_type=F32))
        y = jnp.concatenate(ys, axis=1)

        sel_f = e_ref[0, :, wide]
        sel_b = e_ref[1, :, wide]
        diag_cb = jnp.sum(cm.astype(F32) * bm.astype(F32), axis=-1, keepdims=True)
        dt_b = jnp.dot(dt_bf, sel_b, preferred_element_type=F32)
        y = y + x.astype(F32) * (diag_cb * dt_b + dskip_ref[:, wide])
        ef = jnp.dot(edec, sel_f, preferred_element_type=F32)
        eb = jnp.dot(edec, sel_b, preferred_element_type=F32)
        y = y + jnp.dot(cm, hf_ref[g], preferred_element_type=F32) * ef
        y = y + jnp.dot(cm, hb_ref[g], preferred_element_type=F32) * eb

        z = z_ref[:, wide].astype(F32)
        gz = y * (z * jax.nn.sigmoid(z))
        ms = jnp.mean(gz * gz, axis=-1, keepdims=True)
        o_ref[:, wide] = (gz * lax.rsqrt(ms + EPS) * nw_ref[:, wide]).astype(o_ref.dtype)
        return carry

    lax.fori_loop(0, SSD_GROUPS, one_group, 0)


def _ssd_out(zxbc, xbc, dt, cs, colg, rowg, hf, hb, expand, d_skip, norm_w, d_inner):
    b, lp, _ = xbc.shape
    nc = lp // CHUNK
    w = 2 * HEADS_PER_GROUP
    gn = SSD_GROUPS * SSD_STATE
    assert d_inner == SSD_GROUPS * GROUP_WIDTH and d_inner % gn == 0 and xbc.shape[2] == d_inner + 2 * gn
    st_block = (None, None, SSD_GROUPS, SSD_STATE, GROUP_WIDTH)
    chunk_rows = lambda bi, c: (bi, c, 0)
    per_chunk = lambda bi, c: (bi, c, 0, 0, 0)
    return pl.pallas_call(
        _ssd_out_kernel,
        grid=(b, nc),
        in_specs=[
            pl.BlockSpec((None, CHUNK, d_inner), chunk_rows),
            pl.BlockSpec((None, CHUNK, gn), lambda bi, c: (bi, c, d_inner // gn)),
            pl.BlockSpec((None, CHUNK, gn), lambda bi, c: (bi, c, d_inner // gn + 1)),
            pl.BlockSpec((None, CHUNK, d_inner), chunk_rows),
            pl.BlockSpec((None, CHUNK, dt.shape[2]), chunk_rows),
            pl.BlockSpec((None, CHUNK, cs.shape[2]), chunk_rows),
            pl.BlockSpec((None, None, SSD_GROUPS, CHUNK, w), per_chunk),
            pl.BlockSpec((None, None, SSD_GROUPS, w, CHUNK), per_chunk),
            pl.BlockSpec(st_block, per_chunk),
            pl.BlockSpec(st_block, per_chunk),
            pl.BlockSpec(expand.shape, lambda bi, c: (0, 0, 0)),
            pl.BlockSpec((1, d_inner), lambda bi, c: (0, 0)),
            pl.BlockSpec((1, d_inner), lambda bi, c: (0, 0)),
        ],
        out_specs=pl.BlockSpec((None, CHUNK, d_inner), chunk_rows),
        out_shape=jax.ShapeDtypeStruct((b, lp, d_inner), BF16),
        compiler_params=_params(("parallel", "parallel"), 40),
        name="ssd_out",
    )(xbc, xbc, xbc, zxbc, dt, cs, colg, rowg, hf, hb, expand,
      d_skip.reshape(1, d_inner).astype(F32), norm_w.reshape(1, d_inner).astype(F32))


def _group_major_dt_perm(n_heads):
    perm = []
    for g in range(SSD_GROUPS):
        for direction in range(2):
            for e in range(HEADS_PER_GROUP):
                perm.append(direction * n_heads + g * HEADS_PER_GROUP + e)
    return jnp.array(perm, dtype=jnp.int32)


def _ssd_layer(h, l_tok, layer, norm_w, w_in_stack, conv_w, conv_b, dt_bias, a_log, d_skip,
               gnorm_w, w_out_stack):
    b, lp, d = h.shape
    d_inner = w_out_stack.shape[1]
    n_heads = d_inner // SSD_HEAD_DIM
    n_main = d_inner + conv_w.shape[1]
    perm = _group_major_dt_perm(n_heads)
    h2d = h.reshape(b * lp, d)
    u = _rmsnorm(h2d, norm_w, BF16)
    zxbc = _matmul(u, w_in_stack, layer, n_main, BF16, 1024, 1024).reshape(b, lp, n_main)
    w_dt = jnp.take(w_in_stack[layer, :, n_main:], perm, axis=1)[None]
    dt_raw = _matmul(u, w_dt, 0, 2 * n_heads, F32, 1024, 2 * n_heads).reshape(b, lp, 2 * n_heads)
    xbc = _conv_silu(zxbc, conv_w, conv_b, l_tok, d_inner)
    dt, cs, colg, rowg = _dt_prepare(dt_raw, jnp.take(dt_bias.reshape(-1), perm),
                                     jnp.take(a_log.reshape(-1), perm), l_tok)
    expand = _head_expand_table()
    hf, hb = _ssd_states(xbc, dt, cs, expand, d_inner)
    yn = _ssd_out(zxbc, xbc, dt, cs, colg, rowg, hf, hb, expand,
                  jnp.repeat(d_skip, SSD_HEAD_DIM), gnorm_w, d_inner)
    h2 = _matmul(yn.reshape(b * lp, d_inner), w_out_stack, layer, d, F32, 512, 1024, res=h2d,
                 w_buffers=1, vmem_mb=52)
    return h2.reshape(b, lp, d)


def _ffn_layer(h, layer, norm_w, w_gate_stack, w_up_stack, w_down_stack, u=None):
    b, lp, d = h.shape
    h2d = h.reshape(b * lp, d)
    if u is None:
        u = _rmsnorm(h2d, norm_w, BF16)
    act = _swiglu(u, w_gate_stack, w_up_stack, layer)
    h2 = _matmul(act, w_down_stack, layer, d, F32, 256, 1024, res=h2d, w_buffers=1, vmem_mb=56)
    return h2.reshape(b, lp, d)


def _trunk(x, meta_tokens, norm_mix_w, norm_ffn_w, norm_final_w, fnet_w_out,
           ssd_w_in, ssd_conv_w, ssd_conv_b, ssd_dt_bias, ssd_a_log, ssd_d,
           ssd_norm_w, ssd_w_out, ffn_w_gate, ffn_w_up, ffn_w_down):
    b, seq, d = x.shape
    n_meta = meta_tokens.shape[0]
    l_tok = n_meta + seq
    lp = -(-l_tok // ROW_PAD) * ROW_PAD
    depth = norm_mix_w.shape[0]
    meta = jnp.broadcast_to(meta_tokens.astype(x.dtype)[None], (b, n_meta, d))
    h = jnp.concatenate([meta, x, jnp.zeros((b, lp - l_tok, d), x.dtype)], axis=1)
    hp = -(-(l_tok // 2 + 1) // FNET_TM) * FNET_TM
    chan_tabs = _chan_dft_tables()
    seq_tabs = _seq_dft_tables(l_tok, hp)
    for i in range(depth):
        j = i // 2
        u = None
        if i % 2 == 0:
            h, u = _fourier_layer(h, l_tok, norm_mix_w[i], fnet_w_out, j, norm_ffn_w[i],
                                  chan_tabs, seq_tabs)
        else:
            h = _ssd_layer(h, l_tok, j, norm_mix_w[i], ssd_w_in, ssd_conv_w[j], ssd_conv_b[j],
                           ssd_dt_bias[j], ssd_a_log[j], ssd_d[j], ssd_norm_w[j], ssd_w_out)
        h = _ffn_layer(h, i, norm_ffn_w[i], ffn_w_gate, ffn_w_up, ffn_w_down, u=u)
    return _final_norm(h, norm_final_w, n_meta, seq)


def kernel(x, meta_tokens, norm_mix_w, norm_ffn_w, norm_final_w, fnet_w_out, ssd_w_in, ssd_conv_w, ssd_conv_b, ssd_dt_bias, ssd_a_log, ssd_d, ssd_norm_w, ssd_w_out, ffn_w_gate, ffn_w_up, ffn_w_down):
    return _trunk(x, meta_tokens, norm_mix_w, norm_ffn_w, norm_final_w, fnet_w_out,
                  ssd_w_in, ssd_conv_w, ssd_conv_b, ssd_dt_bias, ssd_a_log, ssd_d,
                  ssd_norm_w, ssd_w_out, ffn_w_gate, ffn_w_up, ffn_w_down)
```

```python
import functools
import math

import jax
import jax.numpy as jnp
from jax import lax
from jax.experimental import pallas as pl
from jax.experimental.pallas import tpu as pltpu

F32 = jnp.float32
BF16 = jnp.bfloat16

N_META = 16
FNET_GROUP_WIDTH = 256
SSD_HEAD_DIM = 64
SSD_GROUPS = 8
SSD_STATE = 128
HEADS_PER_GROUP = 8
GROUP_WIDTH = HEADS_PER_GROUP * SSD_HEAD_DIM
CONV_WIDTH = 5
CHUNK = 256
LOG2E = 1.4426950408889634
HALO = 16
EPS = 1e-6
ROW_PAD = 256


def _params(semantics, vmem_mb):
    return pltpu.CompilerParams(dimension_semantics=semantics,
                                vmem_limit_bytes=vmem_mb * 1024 * 1024)


def _rmsnorm_kernel(h_ref, w_ref, o_ref):
    x = h_ref[...]
    ms = jnp.mean(x * x, axis=-1, keepdims=True)
    o_ref[...] = (x * lax.rsqrt(ms + EPS) * w_ref[...]).astype(o_ref.dtype)


def _rmsnorm(h2d, w, out_dtype, tm=512):
    m, d = h2d.shape
    assert m % tm == 0, (m, tm)
    return pl.pallas_call(
        _rmsnorm_kernel,
        grid=(m // tm,),
        in_specs=[pl.BlockSpec((tm, d), lambda i: (i, 0)),
                  pl.BlockSpec((1, d), lambda i: (0, 0))],
        out_specs=pl.BlockSpec((tm, d), lambda i: (i, 0)),
        out_shape=jax.ShapeDtypeStruct((m, d), out_dtype),
        compiler_params=_params(("parallel",), 40),
        name="rmsnorm",
    )(h2d, w.reshape(1, d).astype(F32))


def _final_norm_kernel(h_ref, nxt_ref, w_ref, o_ref, *, shift):
    def norm(x):
        ms = jnp.mean(x * x, axis=-1, keepdims=True)
        return x * lax.rsqrt(ms + EPS) * w_ref[...]

    tm = o_ref.shape[0]
    o_ref[0:tm - shift, :] = norm(h_ref[shift:, :]).astype(o_ref.dtype)
    o_ref[tm - shift:, :] = norm(nxt_ref[...]).astype(o_ref.dtype)


def _final_norm(h, w, n_skip, seq, tm=256):
    b, lp, d = h.shape
    assert seq % tm == 0 and tm % n_skip == 0 and n_skip % SUBLANES == 0 and lp >= seq + tm, (seq, n_skip, lp)
    per = tm // n_skip
    return pl.pallas_call(
        functools.partial(_final_norm_kernel, shift=n_skip),
        grid=(b, seq // tm),
        in_specs=[pl.BlockSpec((None, tm, d), lambda bi, i: (bi, i, 0)),
                  pl.BlockSpec((None, n_skip, d), lambda bi, i: (bi, (i + 1) * per, 0)),
                  pl.BlockSpec((1, d), lambda bi, i: (0, 0))],
        out_specs=pl.BlockSpec((None, tm, d), lambda bi, i: (bi, i, 0)),
        out_shape=jax.ShapeDtypeStruct((b, seq, d), h.dtype),
        compiler_params=_params(("parallel", "parallel"), 40),
        name="final_norm",
    )(h, h, w.reshape(1, d).astype(F32))


def _mm_kernel(a_ref, w_ref, *rest, has_res):
    if has_res:
        r_ref, o_ref, wbf = rest
    else:
        o_ref, wbf = rest

    @pl.when(pl.program_id(1) == 0)
    def _():
        wbf[...] = w_ref[...].astype(BF16)

    acc = jnp.dot(a_ref[...], wbf[...], preferred_element_type=F32)
    if has_res:
        acc = r_ref[...] + acc
    o_ref[...] = acc.astype(o_ref.dtype)


def _matmul(a, w_stack, layer, n, out_dtype, tm, tn, col0=0, res=None, w_buffers=2, vmem_mb=48):
    m, k = a.shape
    assert w_stack.shape[1] == k, (w_stack.shape, k)
    assert m % tm == 0 and n % tn == 0 and col0 % tn == 0, (m, n, tm, tn, col0)
    cb = col0 // tn
    w_kwargs = {} if w_buffers == 2 else {"pipeline_mode": pl.Buffered(w_buffers)}
    in_specs = [pl.BlockSpec((tm, k), lambda j, i: (i, 0)),
                pl.BlockSpec((None, k, tn), lambda j, i: (layer, 0, cb + j), **w_kwargs)]
    args = [a, w_stack]
    aliases = {}
    if res is not None:
        in_specs.append(pl.BlockSpec((tm, tn), lambda j, i: (i, j)))
        args.append(res)
        aliases = {2: 0}
    return pl.pallas_call(
        functools.partial(_mm_kernel, has_res=res is not None),
        grid=(n // tn, m // tm),
        in_specs=in_specs,
        out_specs=pl.BlockSpec((tm, tn), lambda j, i: (i, j)),
        out_shape=jax.ShapeDtypeStruct((m, n), out_dtype),
        scratch_shapes=[pltpu.VMEM((k, tn), BF16)],
        input_output_aliases=aliases,
        compiler_params=_params(("parallel", "arbitrary"), vmem_mb),
        name="matmul_res" if res is not None else "matmul",
    )(*args)


def _swiglu_kernel(a_ref, wg_ref, wu_ref, o_ref, wg_bf, wu_bf):
    @pl.when(pl.program_id(1) == 0)
    def _():
        wg_bf[...] = wg_ref[...].astype(BF16)
        wu_bf[...] = wu_ref[...].astype(BF16)

    a = a_ref[...]
    gate = jnp.dot(a, wg_bf[...], preferred_element_type=F32)
    up = jnp.dot(a, wu_bf[...], preferred_element_type=F32)
    o_ref[...] = (gate * jax.nn.sigmoid(gate) * up).astype(o_ref.dtype)


def _swiglu(a, wg_stack, wu_stack, layer, tm=1024, tn=512):
    m, k = a.shape
    n = wg_stack.shape[2]
    assert m % tm == 0 and n % tn == 0, (m, n, tm, tn)
    w_spec = pl.BlockSpec((None, k, tn), lambda j, i: (layer, 0, j))
    return pl.pallas_call(
        _swiglu_kernel,
        grid=(n // tn, m // tm),
        in_specs=[pl.BlockSpec((tm, k), lambda j, i: (i, 0)), w_spec, w_spec],
        out_specs=pl.BlockSpec((tm, tn), lambda j, i: (i, j)),
        out_shape=jax.ShapeDtypeStruct((m, n), BF16),
        scratch_shapes=[pltpu.VMEM((k, tn), BF16), pltpu.VMEM((k, tn), BF16)],
        compiler_params=_params(("parallel", "arbitrary"), 48),
        name="swiglu",
    )(a, wg_stack, wu_stack)


FNET_TM = 256


def _mirror_tile(l_tok, i):
    return (l_tok - (i + 1) * FNET_TM) // FNET_TM


def _mirror_select(l_tok):
    r = lax.broadcasted_iota(jnp.int32, (FNET_TM, 2 * FNET_TM), 0)
    c = lax.broadcasted_iota(jnp.int32, (FNET_TM, 2 * FNET_TM), 1)
    return jnp.where(c == (l_tok % FNET_TM) + FNET_TM - r, 1.0, 0.0).astype(BF16)


def _fnet_chan_kernel(h_ref, ha_ref, hb_ref, w_ref, c_ref, s_ref, o_ref, *, l_tok):
    def norm(ref):
        x = ref[...]
        ms = jnp.mean(x * x, axis=-1, keepdims=True)
        return (x * lax.rsqrt(ms + EPS) * w_ref[...]).astype(BF16)

    u = norm(h_ref).astype(F32)
    mirror_src = jnp.concatenate([norm(ha_ref), norm(hb_ref)], axis=0)
    um = jnp.dot(_mirror_select(l_tok), mirror_src, preferred_element_type=F32)
    us = (u + um).astype(BF16)
    ud = (u - um).astype(BF16)
    gw = FNET_GROUP_WIDTH
    for g in range(us.shape[1] // gw):
        cols = slice(g * gw, (g + 1) * gw)
        o_ref[0, :, cols] = jnp.dot(us[:, cols], c_ref[...], preferred_element_type=F32).astype(BF16)
        o_ref[1, :, cols] = jnp.dot(ud[:, cols], s_ref[...], preferred_element_type=F32).astype(BF16)


def _fnet_chan(h, w, c_tab, s_tab, l_tok, hp):
    b, lp, d = h.shape
    tm = FNET_TM
    nh = hp // tm
    assert lp % tm == 0 and hp % tm == 0 and d % FNET_GROUP_WIDTH == 0, (lp, hp, d)
    assert l_tok % 2 == 0 and lp > l_tok and l_tok >= nh * tm and hp > l_tok // 2, (l_tok, lp, hp)
    blk = (None, tm, d)
    return pl.pallas_call(
        functools.partial(_fnet_chan_kernel, l_tok=l_tok),
        grid=(b, nh),
        in_specs=[pl.BlockSpec(blk, lambda bi, i: (bi, i, 0)),
                  pl.BlockSpec(blk, lambda bi, i: (bi, _mirror_tile(l_tok, i), 0)),
                  pl.BlockSpec(blk, lambda bi, i: (bi, _mirror_tile(l_tok, i) + 1, 0)),
                  pl.BlockSpec((1, d), lambda bi, i: (0, 0)),
                  pl.BlockSpec(c_tab.shape, lambda bi, i: (0, 0)),
                  pl.BlockSpec(s_tab.shape, lambda bi, i: (0, 0))],
        out_specs=pl.BlockSpec((None, 2, tm, d), lambda bi, i: (bi, 0, i, 0)),
        out_shape=jax.ShapeDtypeStruct((b, 2, hp, d), BF16),
        compiler_params=_params(("parallel", "parallel"), 48),
        name="fnet_chan",
    )(h, h, h, w.reshape(1, d).astype(F32), c_tab, s_tab)


def _seq_dft_kernel(wc_ref, ws_ref, us_ref, ud_ref, o_ref):
    p = jnp.dot(wc_ref[...], us_ref[...], preferred_element_type=F32)
    q = jnp.dot(ws_ref[...], ud_ref[...], preferred_element_type=F32)
    o_ref[0] = (p - q).astype(o_ref.dtype)
    o_ref[1] = (p + q).astype(o_ref.dtype)


def _seq_dft(wc, ws, ab, tn=1024):
    b, _, hp, d = ab.shape
    tm = next(t for t in (768, 512, 256) if hp % t == 0)
    assert d % tn == 0 and wc.shape == (hp, hp) and ws.shape == (hp, hp), (d, tn, wc.shape)
    return pl.pallas_call(
        _seq_dft_kernel,
        grid=(b, d // tn, hp // tm),
        in_specs=[pl.BlockSpec((tm, hp), lambda bi, j, i: (i, 0)),
                  pl.BlockSpec((tm, hp), lambda bi, j, i: (i, 0)),
                  pl.BlockSpec((None, None, hp, tn), lambda bi, j, i: (bi, 0, 0, j)),
                  pl.BlockSpec((None, None, hp, tn), lambda bi, j, i: (bi, 1, 0, j))],
        out_specs=pl.BlockSpec((None, 2, tm, tn), lambda bi, j, i: (bi, 0, i, j)),
        out_shape=jax.ShapeDtypeStruct((b, 2, hp, d), BF16),
        compiler_params=_params(("parallel", "parallel", "parallel"), 56),
        name="seq_dft",
    )(wc, ws, ab, ab)


def _fnet_out_kernel(fd_ref, fa_ref, fb_ref, w_ref, h_ref, nw_ref, o_ref, u_ref, wbf, *,
                     l_tok, tiles_per_seq):
    t = pl.program_id(1)

    @pl.when(t == 0)
    def _():
        wbf[...] = w_ref[...].astype(BF16)

    tm = FNET_TM
    mirror_src = jnp.concatenate([fa_ref[...], fb_ref[...]], axis=0)
    fm = jnp.dot(_mirror_select(l_tok), mirror_src, preferred_element_type=F32)
    row = (t % tiles_per_seq) * tm + lax.broadcasted_iota(jnp.int32, (tm, 1), 0)
    f = jnp.where(row <= l_tok // 2, fd_ref[...].astype(F32), fm)
    f = jnp.where(row < l_tok, f, 0.0).astype(BF16)
    hn = h_ref[...] + jnp.dot(f, wbf[...], preferred_element_type=F32)
    o_ref[...] = hn
    ms = jnp.mean(hn * hn, axis=-1, keepdims=True)
    u_ref[...] = (hn * lax.rsqrt(ms + EPS) * nw_ref[...]).astype(u_ref.dtype)


def _fnet_out(fm, w_stack, layer, h2d, next_norm_w, l_tok, lp):
    b, _, hp, d = fm.shape
    tm = FNET_TM
    nh = hp // tm
    tps = lp // tm
    n = w_stack.shape[2]
    tn = n
    assert h2d.shape == (b * lp, n), (n, h2d.shape)
    clip = lambda v: jnp.clip(v, 0, nh - 1)
    blk = (None, None, tm, d)
    return pl.pallas_call(
        functools.partial(_fnet_out_kernel, l_tok=l_tok, tiles_per_seq=tps),
        grid=(n // tn, b * tps),
        in_specs=[pl.BlockSpec(blk, lambda j, t: (t // tps, 0, clip(t % tps), 0)),
                  pl.BlockSpec(blk, lambda j, t: (t // tps, 1, clip(_mirror_tile(l_tok, t % tps)), 0)),
                  pl.BlockSpec(blk, lambda j, t: (t // tps, 1, clip(_mirror_tile(l_tok, t % tps) + 1), 0)),
                  pl.BlockSpec((None, d, tn), lambda j, t: (layer, 0, j), pipeline_mode=pl.Buffered(1)),
                  pl.BlockSpec((tm, tn), lambda j, t: (t, j)),
                  pl.BlockSpec((1, tn), lambda j, t: (0, j))],
        out_specs=[pl.BlockSpec((tm, tn), lambda j, t: (t, j)),
                   pl.BlockSpec((tm, tn), lambda j, t: (t, j))],
        out_shape=[jax.ShapeDtypeStruct(h2d.shape, F32), jax.ShapeDtypeStruct(h2d.shape, BF16)],
        scratch_shapes=[pltpu.VMEM((d, tn), BF16)],
        input_output_aliases={4: 0},
        compiler_params=_params(("parallel", "arbitrary"), 56),
        name="fnet_out",
    )(fm, fm, fm, w_stack, h2d, next_norm_w.reshape(1, n).astype(F32))


def _chan_dft_tables():
    n = FNET_GROUP_WIDTH
    j = jnp.arange(n, dtype=jnp.int32)
    th = ((j[:, None] * j[None, :]) % n).astype(F32) * (2.0 * math.pi / n)
    scale = 1.0 / math.sqrt(n)
    return (jnp.cos(th) * scale).astype(BF16), (jnp.sin(th) * scale).astype(BF16)


LANES = 128


def _seq_table_kernel(c1_ref, s1_ref, c2_ref, s2_ref, wc_ref, ws_ref, *, l_tok):
    i = pl.program_id(0)
    tm, hp = wc_ref.shape
    half = l_tok // 2
    scale = 1.0 / math.sqrt(l_tok)
    k = i * tm + lax.broadcasted_iota(jnp.int32, (tm, LANES), 0)
    lane = lax.broadcasted_iota(jnp.int32, (tm, LANES), 1)
    c2 = c2_ref[...]
    s2 = s2_ref[...]
    for a in range(hp // LANES):
        cols = slice(a * LANES, (a + 1) * LANES)
        n = a * LANES + lane
        c1 = c1_ref[:, a:a + 1]
        s1 = s1_ref[:, a:a + 1]
        col_w = jnp.where(n == half, 0.5 * scale, scale)
        wc = jnp.where(k <= half, jnp.where(n <= half, (c1 * c2 - s1 * s2) * col_w, 0.0), 0.0)
        ws = jnp.where(k <= half, jnp.where(n <= half, (s1 * c2 + c1 * s2) * scale, 0.0), 0.0)
        wc_ref[:, cols] = wc.astype(wc_ref.dtype)
        ws_ref[:, cols] = ws.astype(ws_ref.dtype)


def _seq_dft_tables(l_tok, hp):
    tm = FNET_TM
    assert hp % tm == 0 and hp % LANES == 0, hp
    k = jnp.arange(hp, dtype=jnp.int32)[:, None]
    a = jnp.arange(hp // LANES, dtype=jnp.int32)[None, :] * LANES
    b = jnp.arange(LANES, dtype=jnp.int32)[None, :]
    w0 = 2.0 * math.pi / l_tok
    th1 = ((k * a) % l_tok).astype(F32) * w0
    th2 = ((k * b) % l_tok).astype(F32) * w0
    na = hp // LANES
    return pl.pallas_call(
        functools.partial(_seq_table_kernel, l_tok=l_tok),
        grid=(hp // tm,),
        in_specs=[pl.BlockSpec((tm, na), lambda i: (i, 0)),
                  pl.BlockSpec((tm, na), lambda i: (i, 0)),
                  pl.BlockSpec((tm, LANES), lambda i: (i, 0)),
                  pl.BlockSpec((tm, LANES), lambda i: (i, 0))],
        out_specs=[pl.BlockSpec((tm, hp), lambda i: (i, 0)),
                   pl.BlockSpec((tm, hp), lambda i: (i, 0))],
        out_shape=[jax.ShapeDtypeStruct((hp, hp), BF16), jax.ShapeDtypeStruct((hp, hp), BF16)],
        compiler_params=_params(("parallel",), 32),
        name="seq_dft_tables",
    )(jnp.cos(th1), jnp.sin(th1), jnp.cos(th2), jnp.sin(th2))


def _fourier_layer(h, l_tok, norm_w, w_out_stack, layer, next_norm_w, chan_tabs, seq_tabs):
    b, lp, d = h.shape
    hp = seq_tabs[0].shape[0]
    ab = _fnet_chan(h, norm_w, chan_tabs[0], chan_tabs[1], l_tok, hp)
    fm = _seq_dft(seq_tabs[0], seq_tabs[1], ab)
    h2, u = _fnet_out(fm, w_out_stack, layer, h.reshape(b * lp, d), next_norm_w, l_tok, lp)
    return h2.reshape(b, lp, d), u


def _split3(x):
    hi = x.astype(BF16)
    r = x - hi.astype(F32)
    mid = r.astype(BF16)
    lo = (r - mid.astype(F32)).astype(BF16)
    return hi, mid, lo


CONV_SUB = 512
SUBLANES = 8


def _conv_kernel(prev_ref, cur_ref, next_ref, shift_ref, w_ref, b_ref, o_ref, *, l_tok):
    i = pl.program_id(1)
    n = pl.num_programs(1)
    rows = cur_ref.shape[0]
    half = CONV_WIDTH // 2
    offsets = [d for d in range(-half, half + 1) if d != 0]
    row = i * rows + lax.broadcasted_iota(jnp.int32, (rows, CONV_SUB), 0)
    sub = lax.broadcasted_iota(jnp.int32, (SUBLANES, CONV_SUB), 0)
    for s in range(cur_ref.shape[1] // CONV_SUB):
        cols = slice(s * CONV_SUB, (s + 1) * CONV_SUB)
        xc = cur_ref[:, cols]
        shifted = jnp.dot(shift_ref[...], xc, preferred_element_type=F32)
        acc = xc.astype(F32) * w_ref[half:half + 1, cols] + b_ref[:, cols]
        for t, d in enumerate(offsets):
            acc = acc + shifted[t * rows:(t + 1) * rows] * w_ref[half + d:half + d + 1, cols]
        prev = jnp.where(i > 0, prev_ref[:, cols].astype(F32)[HALO - SUBLANES:, :], 0.0)
        nxt = jnp.where(i < n - 1, next_ref[:, cols].astype(F32)[:SUBLANES, :], 0.0)
        top = jnp.zeros((SUBLANES, CONV_SUB), F32)
        bot = jnp.zeros((SUBLANES, CONV_SUB), F32)
        for d in range(1, half + 1):
            top = top + jnp.where(sub < d, pltpu.roll(prev, d, 0), 0.0) * w_ref[half - d:half - d + 1, cols]
            bot = bot + jnp.where(sub >= SUBLANES - d, pltpu.roll(nxt, SUBLANES - d, 0), 0.0) \
                * w_ref[half + d:half + d + 1, cols]
        acc = jnp.concatenate([acc[:SUBLANES] + top, acc[SUBLANES:rows - SUBLANES],
                               acc[rows - SUBLANES:] + bot], axis=0)
        y = acc * jax.nn.sigmoid(acc)
        o_ref[:, cols] = jnp.where(row < l_tok, y, 0.0).astype(o_ref.dtype)


def _conv_shift_table(rows):
    half = CONV_WIDTH // 2
    r = jnp.arange(rows, dtype=jnp.int32)[:, None]
    c = jnp.arange(rows, dtype=jnp.int32)[None, :]
    blocks = [jnp.where(c == r + d, 1.0, 0.0) for d in range(-half, half + 1) if d != 0]
    return jnp.concatenate(blocks, axis=0).astype(BF16)


def _conv_silu(zxbc, conv_w, conv_b, l_tok, d_inner, tc=2048):
    b, lp, _ = zxbc.shape
    conv_dim = conv_w.shape[1]
    assert lp % CHUNK == 0 and conv_dim % tc == 0 and d_inner % tc == 0, (lp, conv_dim, d_inner)
    assert tc % CONV_SUB == 0 and HALO >= SUBLANES >= CONV_WIDTH // 2
    c0 = d_inner // tc
    hb = CHUNK // HALO
    nh = lp // HALO
    return pl.pallas_call(
        functools.partial(_conv_kernel, l_tok=l_tok),
        grid=(b, lp // CHUNK, conv_dim // tc),
        in_specs=[
            pl.BlockSpec((None, HALO, tc), lambda bi, i, j: (bi, jnp.maximum(i * hb - 1, 0), c0 + j)),
            pl.BlockSpec((None, CHUNK, tc), lambda bi, i, j: (bi, i, c0 + j)),
            pl.BlockSpec((None, HALO, tc), lambda bi, i, j: (bi, jnp.minimum((i + 1) * hb, nh - 1), c0 + j)),
            pl.BlockSpec(((CONV_WIDTH - 1) * CHUNK, CHUNK), lambda bi, i, j: (0, 0)),
            pl.BlockSpec((CONV_WIDTH, tc), lambda bi, i, j: (0, j)),
            pl.BlockSpec((1, tc), lambda bi, i, j: (0, j)),
        ],
        out_specs=pl.BlockSpec((None, CHUNK, tc), lambda bi, i, j: (bi, i, j)),
        out_shape=jax.ShapeDtypeStruct((b, lp, conv_dim), BF16),
        compiler_params=_params(("parallel", "parallel", "parallel"), 32),
        name="conv_silu",
    )(zxbc, zxbc, zxbc, _conv_shift_table(CHUNK), conv_w.astype(F32),
      conv_b.reshape(1, conv_dim).astype(F32))


def _dt_kernel(raw_ref, bias_ref, alog_ref, dt_ref, cs_ref, colg_ref, rowg_ref, *, l_tok):
    c = pl.program_id(1)
    x = raw_ref[...] + bias_ref[...]
    dt = jnp.maximum(x, 0.0) + jnp.log1p(jnp.exp(-jnp.abs(x)))
    row = c * CHUNK + lax.broadcasted_iota(jnp.int32, x.shape, 0)
    dt = jnp.where(row < l_tok, dt, 0.0)
    da = dt * (-jnp.exp(alog_ref[...]))
    li = lax.broadcasted_iota(jnp.int32, (CHUNK, CHUNK), 0)
    si = lax.broadcasted_iota(jnp.int32, (CHUNK, CHUNK), 1)
    tri_l = jnp.where(li >= si, 1.0, 0.0).astype(BF16)
    tri_u = jnp.where(li <= si, 1.0, 0.0).astype(BF16)
    hi, mid, lo = _split3(da)
    prefix = (jnp.dot(tri_l, hi, preferred_element_type=F32)
              + jnp.dot(tri_l, mid, preferred_element_type=F32)
              + jnp.dot(tri_l, lo, preferred_element_type=F32))
    suffix = (jnp.dot(tri_u, hi, preferred_element_type=F32)
              + jnp.dot(tri_u, mid, preferred_element_type=F32)
              + jnp.dot(tri_u, lo, preferred_element_type=F32))
    col = lax.broadcasted_iota(jnp.int32, x.shape, 1)
    cs = jnp.where((col & HEADS_PER_GROUP) != 0, suffix, prefix) * LOG2E
    dt_ref[...] = dt
    cs_ref[...] = cs
    src_t = (cs - jnp.log2(dt)).T
    w = 2 * HEADS_PER_GROUP
    for g in range(SSD_GROUPS):
        colg_ref[g] = cs[:, g * w:(g + 1) * w]
        rowg_ref[g] = src_t[g * w:(g + 1) * w, :]


def _dt_prepare(dt_raw, dt_bias, a_log, l_tok):
    b, lp, nh2 = dt_raw.shape
    assert lp % CHUNK == 0 and nh2 == 2 * HEADS_PER_GROUP * SSD_GROUPS, (lp, nh2)
    nc = lp // CHUNK
    w = 2 * HEADS_PER_GROUP
    blk = lambda bi, c: (bi, c, 0)
    return pl.pallas_call(
        functools.partial(_dt_kernel, l_tok=l_tok),
        grid=(b, nc),
        in_specs=[pl.BlockSpec((None, CHUNK, nh2), blk),
                  pl.BlockSpec((1, nh2), lambda bi, c: (0, 0)),
                  pl.BlockSpec((1, nh2), lambda bi, c: (0, 0))],
        out_specs=[pl.BlockSpec((None, CHUNK, nh2), blk),
                   pl.BlockSpec((None, CHUNK, nh2), blk),
                   pl.BlockSpec((None, None, SSD_GROUPS, CHUNK, w), lambda bi, c: (bi, c, 0, 0, 0)),
                   pl.BlockSpec((None, None, SSD_GROUPS, w, CHUNK), lambda bi, c: (bi, c, 0, 0, 0))],
        out_shape=[jax.ShapeDtypeStruct((b, lp, nh2), F32),
                   jax.ShapeDtypeStruct((b, lp, nh2), F32),
                   jax.ShapeDtypeStruct((b, nc, SSD_GROUPS, CHUNK, w), F32),
                   jax.ShapeDtypeStruct((b, nc, SSD_GROUPS, w, CHUNK), F32)],
        compiler_params=_params(("parallel", "parallel"), 32),
        name="dt_prepare",
    )(dt_raw, dt_bias.reshape(1, nh2).astype(F32), a_log.reshape(1, nh2).astype(F32))


def _state_kernel(xf_ref, bf_ref, dtf_ref, csf_ref, xb_ref, bb_ref, dtb_ref, csb_ref, e_ref,
                  hf_out, hb_out, hf, hb):
    t = pl.program_id(1)

    @pl.when(t == 0)
    def _():
        hf[...] = jnp.zeros_like(hf)
        hb[...] = jnp.zeros_like(hb)

    lane8 = lax.broadcasted_iota(jnp.int32, (SUBLANES, SSD_STATE), 1)

    def one_direction(x_ref, b_ref, dt_ref, cs_ref, h_out, h, total_row, d):
        cs = cs_ref[...]
        total = cs[total_row:total_row + 1, :]
        col = lax.broadcasted_iota(jnp.int32, cs.shape, 1)
        mine = (col & HEADS_PER_GROUP) == d * HEADS_PER_GROUP
        scale = (dt_ref[...] * jnp.exp2(jnp.where(mine, total - cs, 0.0))).astype(BF16)
        decay8 = jnp.broadcast_to(jnp.exp2(total), (SUBLANES, cs.shape[1]))
        xdec = x_ref[...] * jnp.dot(scale, e_ref[d], preferred_element_type=F32).astype(BF16)
        for g in range(SSD_GROUPS):
            s_new = lax.dot_general(b_ref[:, g * SSD_STATE:(g + 1) * SSD_STATE],
                                    xdec[:, g * GROUP_WIDTH:(g + 1) * GROUP_WIDTH],
                                    (((0,), (0,)), ((), ())), preferred_element_type=F32)
            first = (2 * g + d) * HEADS_PER_GROUP
            dec = jnp.concatenate(
                [jnp.take_along_axis(decay8, first + 2 * j + (lane8 >> 6), axis=1)
                 for j in range(GROUP_WIDTH // SSD_STATE)], axis=1)[0:1, :]
            h_prev = h[g]
            h_out[g] = h_prev.astype(h_out.dtype)
            h[g] = h_prev * dec + s_new

    one_direction(xf_ref, bf_ref, dtf_ref, csf_ref, hf_out, hf, CHUNK - 1, 0)
    one_direction(xb_ref, bb_ref, dtb_ref, csb_ref, hb_out, hb, 0, 1)


def _head_expand_table():
    d = jnp.arange(2, dtype=jnp.int32)[:, None, None]
    r = jnp.arange(2 * SSD_GROUPS * HEADS_PER_GROUP, dtype=jnp.int32)[None, :, None]
    c = jnp.arange(SSD_GROUPS * GROUP_WIDTH, dtype=jnp.int32)[None, None, :]
    src = (2 * (c // GROUP_WIDTH) + d) * HEADS_PER_GROUP + (c % GROUP_WIDTH) // SSD_HEAD_DIM
    return jnp.where(r == src, 1.0, 0.0).astype(BF16)


def _ssd_states(xbc, dt, cs, expand, d_inner):
    b, lp, _ = xbc.shape
    nc = lp // CHUNK
    gn = SSD_GROUPS * SSD_STATE
    nh2 = dt.shape[2]
    fwd = lambda bi, t: (bi, t, 0)
    bwd = lambda bi, t: (bi, nc - 1 - t, 0)
    fwd_b = lambda bi, t: (bi, t, d_inner // gn)
    bwd_b = lambda bi, t: (bi, nc - 1 - t, d_inner // gn)
    st_shape = jax.ShapeDtypeStruct((b, nc, SSD_GROUPS, SSD_STATE, GROUP_WIDTH), BF16)
    st_block = (None, None, SSD_GROUPS, SSD_STATE, GROUP_WIDTH)
    return pl.pallas_call(
        _state_kernel,
        grid=(b, nc),
        in_specs=[pl.BlockSpec((None, CHUNK, d_inner), fwd),
                  pl.BlockSpec((None, CHUNK, gn), fwd_b),
                  pl.BlockSpec((None, CHUNK, nh2), fwd),
                  pl.BlockSpec((None, CHUNK, nh2), fwd),
                  pl.BlockSpec((None, CHUNK, d_inner), bwd),
                  pl.BlockSpec((None, CHUNK, gn), bwd_b),
                  pl.BlockSpec((None, CHUNK, nh2), bwd),
                  pl.BlockSpec((None, CHUNK, nh2), bwd),
                  pl.BlockSpec(expand.shape, lambda bi, t: (0, 0, 0))],
        out_specs=[pl.BlockSpec(st_block, lambda bi, t: (bi, t, 0, 0, 0)),
                   pl.BlockSpec(st_block, lambda bi, t: (bi, nc - 1 - t, 0, 0, 0))],
        out_shape=[st_shape, st_shape],
        scratch_shapes=[pltpu.VMEM((SSD_GROUPS, SSD_STATE, GROUP_WIDTH), F32),
                        pltpu.VMEM((SSD_GROUPS, SSD_STATE, GROUP_WIDTH), F32)],
        compiler_params=_params(("parallel", "arbitrary"), 40),
        name="ssd_states",
    )(xbc, xbc, dt, cs, xbc, xbc, dt, cs, expand)


def _ssd_out_kernel(x_ref, b_ref, c_ref, z_ref, dt_ref, cs_ref, colg_ref, rowg_ref, hf_ref, hb_ref,
                    e_ref, dskip_ref, nw_ref, o_ref):
    hpg = HEADS_PER_GROUP
    half = CHUNK // 2
    lower = (lax.broadcasted_iota(jnp.int32, (half, half), 0)
             >= lax.broadcasted_iota(jnp.int32, (half, half), 1))
    lane = lax.broadcasted_iota(jnp.int32, (CHUNK, 2 * SSD_HEAD_DIM), 1)
    zero_bf = jnp.zeros((CHUNK, 2 * SSD_HEAD_DIM), BF16)
    edec = jnp.exp2(cs_ref[...]).astype(BF16)
    dt_bf = dt_ref[...].astype(BF16)

    def one_group(g, carry):
        wide = pl.ds(pl.multiple_of(g * GROUP_WIDTH, GROUP_WIDTH), GROUP_WIDTH)
        narrow = pl.ds(pl.multiple_of(g * SSD_STATE, SSD_STATE), SSD_STATE)
        x = x_ref[:, wide]
        cm = c_ref[:, narrow]
        bm = b_ref[:, narrow]
        cb = lax.dot_general(cm, bm, (((1,), (1,)), ((), ())),
                             preferred_element_type=F32).astype(BF16)
        colg = colg_ref[g]
        rowg = rowg_ref[g]
        ys = []
        for p in range(hpg // 2):
            gms = []
            for hh in (2 * p, 2 * p + 1):
                cf = colg[:, hh:hh + 1]
                sb = colg[:, hpg + hh:hpg + hh + 1]
                rf = rowg[hh:hh + 1, :]
                rb = rowg[hpg + hh:hpg + hh + 1, :]
                top = jnp.concatenate(
                    [jnp.where(lower, cf[:half] - rf[:, :half], sb[:half] - rb[:, :half]),
                     sb[:half] - rb[:, half:]], axis=1)
                bot = jnp.concatenate(
                    [cf[half:] - rf[:, :half],
                     jnp.where(lower, cf[half:] - rf[:, half:], sb[half:] - rb[:, half:])], axis=1)
                arg = jnp.concatenate([top, bot], axis=0)
                gms.append(jnp.exp2(arg).astype(BF16) * cb)
            g2 = jnp.concatenate(gms, axis=1)
            xp = x[:, p * 2 * SSD_HEAD_DIM:(p + 1) * 2 * SSD_HEAD_DIM]
            xbd = jnp.concatenate([jnp.where(lane < SSD_HEAD_DIM, xp, zero_bf),
                                   jnp.where(lane >= SSD_HEAD_DIM, xp, zero_bf)], axis=0)
            ys.append(jnp.dot(g2, xbd, preferred_element_type=F32))
        y = jnp.concatenate(ys, axis=1)

        sel_f = e_ref[0, :, wide]
        sel_b = e_ref[1, :, wide]
        diag_cb = jnp.sum(cm.astype(F32) * bm.astype(F32), axis=-1, keepdims=True)
        dt_b = jnp.dot(dt_bf, sel_b, preferred_element_type=F32)
        y = y + x.astype(F32) * (diag_cb * dt_b + dskip_ref[:, wide])
        ef = jnp.dot(edec, sel_f, preferred_element_type=F32)
        eb = jnp.dot(edec, sel_b, preferred_element_type=F32)
        y = y + jnp.dot(cm, hf_ref[g], preferred_element_type=F32) * ef
        y = y + jnp.dot(cm, hb_ref[g], preferred_element_type=F32) * eb

        z = z_ref[:, wide].astype(F32)
        gz = y * (z * jax.nn.sigmoid(z))
        ms = jnp.mean(gz * gz, axis=-1, keepdims=True)
        o_ref[:, wide] = (gz * lax.rsqrt(ms + EPS) * nw_ref[:, wide]).astype(o_ref.dtype)
        return carry

    lax.fori_loop(0, SSD_GROUPS, one_group, 0)


def _ssd_out(zxbc, xbc, dt, cs, colg, rowg, hf, hb, expand, d_skip, norm_w, d_inner):
    b, lp, _ = xbc.shape
    nc = lp // CHUNK
    w = 2 * HEADS_PER_GROUP
    gn = SSD_GROUPS * SSD_STATE
    assert d_inner == SSD_GROUPS * GROUP_WIDTH and d_inner % gn == 0 and xbc.shape[2] == d_inner + 2 * gn
    st_block = (None, None, SSD_GROUPS, SSD_STATE, GROUP_WIDTH)
    chunk_rows = lambda bi, c: (bi, c, 0)
    per_chunk = lambda bi, c: (bi, c, 0, 0, 0)
    return pl.pallas_call(
        _ssd_out_kernel,
        grid=(b, nc),
        in_specs=[
            pl.BlockSpec((None, CHUNK, d_inner), chunk_rows),
            pl.BlockSpec((None, CHUNK, gn), lambda bi, c: (bi, c, d_inner // gn)),
            pl.BlockSpec((None, CHUNK, gn), lambda bi, c: (bi, c, d_inner // gn + 1)),
            pl.BlockSpec((None, CHUNK, d_inner), chunk_rows),
            pl.BlockSpec((None, CHUNK, dt.shape[2]), chunk_rows),
            pl.BlockSpec((None, CHUNK, cs.shape[2]), chunk_rows),
            pl.BlockSpec((None, None, SSD_GROUPS, CHUNK, w), per_chunk),
            pl.BlockSpec((None, None, SSD_GROUPS, w, CHUNK), per_chunk),
            pl.BlockSpec(st_block, per_chunk),
            pl.BlockSpec(st_block, per_chunk),
            pl.BlockSpec(expand.shape, lambda bi, c: (0, 0, 0)),
            pl.BlockSpec((1, d_inner), lambda bi, c: (0, 0)),
            pl.BlockSpec((1, d_inner), lambda bi, c: (0, 0)),
        ],
        out_specs=pl.BlockSpec((None, CHUNK, d_inner), chunk_rows),
        out_shape=jax.ShapeDtypeStruct((b, lp, d_inner), BF16),
        compiler_params=_params(("parallel", "parallel"), 40),
        name="ssd_out",
    )(xbc, xbc, xbc, zxbc, dt, cs, colg, rowg, hf, hb, expand,
      d_skip.reshape(1, d_inner).astype(F32), norm_w.reshape(1, d_inner).astype(F32))


def _group_major_dt_perm(n_heads):
    perm = []
    for g in range(SSD_GROUPS):
        for direction in range(2):
            for e in range(HEADS_PER_GROUP):
                perm.append(direction * n_heads + g * HEADS_PER_GROUP + e)
    return jnp.array(perm, dtype=jnp.int32)


def _ssd_layer(h, l_tok, layer, norm_w, w_in_stack, conv_w, conv_b, dt_bias, a_log, d_skip,
               gnorm_w, w_out_stack):
    b, lp, d = h.shape
    d_inner = w_out_stack.shape[1]
    n_heads = d_inner // SSD_HEAD_DIM
    n_main = d_inner + conv_w.shape[1]
    perm = _group_major_dt_perm(n_heads)
    h2d = h.reshape(b * lp, d)
    u = _rmsnorm(h2d, norm_w, BF16)
    zxbc = _matmul(u, w_in_stack, layer, n_main, BF16, 1024, 1024).reshape(b, lp, n_main)
    w_dt = jnp.take(w_in_stack[layer, :, n_main:], perm, axis=1)[None]
    dt_raw = _matmul(u, w_dt, 0, 2 * n_heads, F32, 1024, 2 * n_heads).reshape(b, lp, 2 * n_heads)
    xbc = _conv_silu(zxbc, conv_w, conv_b, l_tok, d_inner)
    dt, cs, colg, rowg = _dt_prepare(dt_raw, jnp.take(dt_bias.reshape(-1), perm),
                                     jnp.take(a_log.reshape(-1), perm), l_tok)
    expand = _head_expand_table()
    hf, hb = _ssd_states(xbc, dt, cs, expand, d_inner)
    yn = _ssd_out(zxbc, xbc, dt, cs, colg, rowg, hf, hb, expand,
                  jnp.repeat(d_skip, SSD_HEAD_DIM), gnorm_w, d_inner)
    h2 = _matmul(yn.reshape(b * lp, d_inner), w_out_stack, layer, d, F32, 512, 1024, res=h2d,
                 w_buffers=1, vmem_mb=52)
    return h2.reshape(b, lp, d)


def _ffn_layer(h, layer, norm_w, w_gate_stack, w_up_stack, w_down_stack, u=None):
    b, lp, d = h.shape
    h2d = h.reshape(b * lp, d)
    if u is None:
        u = _rmsnorm(h2d, norm_w, BF16)
    act = _swiglu(u, w_gate_stack, w_up_stack, layer)
    h2 = _matmul(act, w_down_stack, layer, d, F32, 256, 1024, res=h2d, w_buffers=1, vmem_mb=56)
    return h2.reshape(b, lp, d)


def _trunk(x, meta_tokens, norm_mix_w, norm_ffn_w, norm_final_w, fnet_w_out,
           ssd_w_in, ssd_conv_w, ssd_conv_b, ssd_dt_bias, ssd_a_log, ssd_d,
           ssd_norm_w, ssd_w_out, ffn_w_gate, ffn_w_up, ffn_w_down):
    b, seq, d = x.shape
    n_meta = meta_tokens.shape[0]
    l_tok = n_meta + seq
    lp = -(-l_tok // ROW_PAD) * ROW_PAD
    depth = norm_mix_w.shape[0]
    meta = jnp.broadcast_to(meta_tokens.astype(x.dtype)[None], (b, n_meta, d))
    h = jnp.concatenate([meta, x, jnp.zeros((b, lp - l_tok, d), x.dtype)], axis=1)
    hp = -(-(l_tok // 2 + 1) // FNET_TM) * FNET_TM
    chan_tabs = _chan_dft_tables()
    seq_tabs = _seq_dft_tables(l_tok, hp)
    for i in range(depth):
        j = i // 2
        u = None
        if i % 2 == 0:
            h, u = _fourier_layer(h, l_tok, norm_mix_w[i], fnet_w_out, j, norm_ffn_w[i],
                                  chan_tabs, seq_tabs)
        else:
            h = _ssd_layer(h, l_tok, j, norm_mix_w[i], ssd_w_in, ssd_conv_w[j], ssd_conv_b[j],
                           ssd_dt_bias[j], ssd_a_log[j], ssd_d[j], ssd_norm_w[j], ssd_w_out)
        h = _ffn_layer(h, i, norm_ffn_w[i], ffn_w_gate, ffn_w_up, ffn_w_down, u=u)
    return _final_norm(h, norm_final_w, n_meta, seq)


def kernel(x, meta_tokens, norm_mix_w, norm_ffn_w, norm_final_w, fnet_w_out, ssd_w_in, ssd_conv_w, ssd_conv_b, ssd_dt_bias, ssd_a_log, ssd_d, ssd_norm_w, ssd_w_out, ffn_w_gate, ffn_w_up, ffn_w_down):
    return _trunk(x, meta_tokens, norm_mix_w, norm_ffn_w, norm_final_w, fnet_w_out,
                  ssd_w_in, ssd_conv_w, ssd_conv_b, ssd_dt_bias, ssd_a_log, ssd_d,
                  ssd_norm_w, ssd_w_out, ffn_w_gate, ffn_w_up, ffn_w_down)
```
